```python
import jax, jax.numpy as jnp
from jax import lax
import numpy as np


D_MODEL = 1024
BATCH = 32
SEQ = 2048
DEPTH = 1

ATTN_WIDTH = D_MODEL // 2
HEAD_DIM = 64
N_HEADS = ATTN_WIDTH // HEAD_DIM
POOL_WIDTH = D_MODEL - ATTN_WIDTH
POOL_WINDOWS = (2, 4, 8, 16)
N_POOL_GROUPS = len(POOL_WINDOWS)
POOL_GROUP = POOL_WIDTH // N_POOL_GROUPS
IN_WIDTH = 3 * ATTN_WIDTH + POOL_WIDTH
MOBA_BLOCK = 256
MOBA_TOPK = 3
QUERY_CHUNK = 16
ROPE_THETA = 10000.0
D_FF = 4 * D_MODEL
NORM_EPS = 1e-6

kernel_name = 'hymba_moba_pool_sqrelu_adaln'


def rmsnorm(x, g):
    x32 = x.astype(jnp.float32)
    y = x32 * lax.rsqrt(jnp.mean(x32 * x32, axis=-1, keepdims=True) + NORM_EPS)
    return (y * g.astype(jnp.float32)).astype(x.dtype)


def rope(x, pos):
    half = HEAD_DIM // 2
    inv_freq = 1.0 / (ROPE_THETA ** (jnp.arange(half, dtype=jnp.float32) * (2.0 / HEAD_DIM)))
    ang = pos.astype(jnp.float32)[:, None] * inv_freq[None, :]
    cos = jnp.cos(ang)[None, :, None, :]
    sin = jnp.sin(ang)[None, :, None, :]
    x32 = x.astype(jnp.float32)
    x1, x2 = x32[..., :half], x32[..., half:]
    out = jnp.concatenate([x1 * cos - x2 * sin, x2 * cos + x1 * sin], axis=-1)
    return out.astype(x.dtype)


def moba_attention(q, k, v):
    B, S = q.shape[0], q.shape[1]
    nb = -(-S // MOBA_BLOCK)
    s_pad = nb * MOBA_BLOCK
    k_eff = max(1, min(MOBA_TOPK, nb - 1))
    scale = HEAD_DIM ** -0.5
    qh = q.transpose(0, 2, 1, 3)
    pad = ((0, 0), (0, 0), (0, s_pad - S), (0, 0))
    kb = jnp.pad(k.transpose(0, 2, 1, 3), pad).reshape(B, N_HEADS, nb, MOBA_BLOCK, HEAD_DIM)
    vb = jnp.pad(v.transpose(0, 2, 1, 3), pad).reshape(B, N_HEADS, nb, MOBA_BLOCK, HEAD_DIM)
    k_mean = jnp.mean(kb.astype(jnp.float32), axis=3)
    n_chunks = S // QUERY_CHUNK
    q_chunks = qh.reshape(B, N_HEADS, n_chunks, QUERY_CHUNK, HEAD_DIM).transpose(2, 0, 1, 3, 4)
    bi = jnp.arange(B)[:, None, None, None]
    hi = jnp.arange(N_HEADS)[None, :, None, None]
    key_in_block = jnp.arange(MOBA_BLOCK)
    block_ids = jnp.arange(nb)

    def one_chunk(args):
        q_c, ci = args
        start = ci * QUERY_CHUNK
        blk = start // MOBA_BLOCK
        t = start + jnp.arange(QUERY_CHUNK)
        gate = jnp.einsum('bhqd,bhnd->bhqn', q_c.astype(jnp.float32), k_mean)
        gate = jnp.where(block_ids < blk, gate, -jnp.inf)
        _, idx = lax.top_k(gate, k_eff)
        valid = idx < blk
        k_sel = kb[bi, hi, idx]
        v_sel = vb[bi, hi, idx]
        s_sel = jnp.einsum('bhqd,bhqkpd->bhqkp', q_c, k_sel).astype(jnp.float32) * scale
        s_sel = jnp.where(valid[..., None], s_sel, -jnp.inf)
        s_sel = s_sel.reshape(B, N_HEADS, QUERY_CHUNK, k_eff * MOBA_BLOCK)
        k_own = lax.dynamic_index_in_dim(kb, blk, axis=2, keepdims=False)
        v_own = lax.dynamic_index_in_dim(vb, blk, axis=2, keepdims=False)
        s_own = jnp.einsum('bhqd,bhpd->bhqp', q_c, k_own).astype(jnp.float32) * scale
        causal = (blk * MOBA_BLOCK + key_in_block)[None, :] <= t[:, None]
        s_own = jnp.where(causal, s_own, -jnp.inf)
        p = jax.nn.softmax(jnp.concatenate([s_own, s_sel], axis=-1), axis=-1).astype(v.dtype)
        p_own = p[..., :MOBA_BLOCK]
        p_sel = p[..., MOBA_BLOCK:].reshape(B, N_HEADS, QUERY_CHUNK, k_eff, MOBA_BLOCK)
        return (jnp.einsum('bhqp,bhpd->bhqd', p_own, v_own)
                + jnp.einsum('bhqkp,bhqkpd->bhqd', p_sel, v_sel))

    out = lax.map(one_chunk, (q_chunks, jnp.arange(n_chunks)))
    return out.transpose(1, 0, 3, 2, 4).reshape(B, S, ATTN_WIDTH)


def pool_mixer(u, w_pool, pool_scale):
    B, S = u.shape[0], u.shape[1]
    ug = u.reshape(B, S, N_POOL_GROUPS, POOL_GROUP).astype(jnp.float32)
    cs = jnp.cumsum(ug, axis=1)
    pos = jnp.arange(S)
    diffs = []
    for g, w in enumerate(POOL_WINDOWS):
        c_g = cs[:, :, g]
        window_sum = c_g - jnp.pad(c_g, ((0, 0), (w, 0), (0, 0)))[:, :S]
        count = jnp.minimum(pos + 1, w).astype(jnp.float32)[None, :, None]
        diffs.append(window_sum / count - ug[:, :, g])
    d = jnp.stack(diffs, axis=2).astype(u.dtype)
    y = jnp.einsum('bsgc,gce->bsge', d, w_pool).reshape(B, S, POOL_WIDTH)
    return y * pool_scale


def setup_inputs(seed: int = 0) -> dict:
    key = jax.random.key(seed)
    ks = jax.random.split(key, 16)
    f32 = jnp.float32
    D = D_MODEL
    def nrm(k, shape, s):
        return jax.random.normal(k, shape, f32) * s
    return {
        'x': nrm(ks[0], (BATCH, SEQ, D), 1.0),
        'c': nrm(ks[1], (BATCH, D), 1.0),
        'w_ada': nrm(ks[2], (DEPTH, D, 6 * D), D ** -0.5),
        'b_ada': nrm(ks[3], (DEPTH, 6 * D), 0.02),
        'g_mix_pre': 1.0 + nrm(ks[4], (DEPTH, D), 0.02),
        'g_mix_post': 1.0 + nrm(ks[5], (DEPTH, D), 0.02),
        'w_in': nrm(ks[6], (DEPTH, D, IN_WIDTH), D ** -0.5),
        'w_pool': nrm(ks[7], (DEPTH, N_POOL_GROUPS, POOL_GROUP, POOL_GROUP), POOL_GROUP ** -0.5),
        'pool_scale': 1.0 + nrm(ks[8], (DEPTH, POOL_WIDTH), 0.02),
        'w_out': nrm(ks[9], (DEPTH, ATTN_WIDTH + POOL_WIDTH, D), (ATTN_WIDTH + POOL_WIDTH) ** -0.5),
        'g_mlp_pre': 1.0 + nrm(ks[10], (DEPTH, D), 0.02),
        'g_mlp_post': 1.0 + nrm(ks[11], (DEPTH, D), 0.02),
        'w_up': nrm(ks[12], (DEPTH, D, D_FF), D ** -0.5),
        'w_down': nrm(ks[13], (DEPTH, D_FF, D), D_FF ** -0.5),
    }


def reference(x, c, w_ada, b_ada, g_mix_pre, g_mix_post, w_in, w_pool, pool_scale, w_out,
              g_mlp_pre, g_mlp_post, w_up, w_down):
    B, S = x.shape[0], x.shape[1]
    pos = jnp.arange(S, dtype=jnp.int32)
    c_act = jax.nn.silu(c)
    for l in range(DEPTH):
        mod = c_act @ w_ada[l] + b_ada[l]
        sh1, sc1, ga1, sh2, sc2, ga2 = [m[:, None, :] for m in jnp.split(mod, 6, axis=-1)]
        h = rmsnorm(x, g_mix_pre[l]) * (1.0 + sc1) + sh1
        proj = h @ w_in[l]
        q = proj[..., :ATTN_WIDTH].reshape(B, S, N_HEADS, HEAD_DIM)
        k = proj[..., ATTN_WIDTH:2 * ATTN_WIDTH].reshape(B, S, N_HEADS, HEAD_DIM)
        v = proj[..., 2 * ATTN_WIDTH:3 * ATTN_WIDTH].reshape(B, S, N_HEADS, HEAD_DIM)
        u = proj[..., 3 * ATTN_WIDTH:]
        attn_out = moba_attention(rope(q, pos), rope(k, pos), v)
        pool_out = pool_mixer(u, w_pool[l], pool_scale[l])
        y = jnp.concatenate([attn_out, pool_out], axis=-1) @ w_out[l]
        x = x + ga1 * rmsnorm(y, g_mix_post[l])
        h = rmsnorm(x, g_mlp_pre[l]) * (1.0 + sc2) + sh2
        y = jnp.square(jax.nn.relu(h @ w_up[l])) @ w_down[l]
        x = x + ga2 * rmsnorm(y, g_mlp_post[l])
    return x
```

```python
import functools

import jax
import jax.numpy as jnp
from jax import lax
from jax.experimental import pallas as pl
from jax.experimental.pallas import tpu as pltpu

F32 = jnp.float32
BF16 = jnp.bfloat16

HEAD_DIM = 64
HALF = HEAD_DIM // 2
MOBA_BLOCK = 256
MOBA_TOPK = 3
POOL_WINDOWS = (2, 4, 8, 16)
POOL_GROUP = 128
MAX_WINDOW = max(POOL_WINDOWS)
ROPE_THETA = 10000.0
NORM_EPS = 1e-6
LANES = 128
VMEM_LIMIT = 56 * 1024 * 1024

NT_DIMS = (((1,), (1,)), ((), ()))


def _rms(x):
    return x * lax.rsqrt(jnp.mean(x * x, axis=-1, keepdims=True) + NORM_EPS)


def _mod_kernel(c_ref, w_ref, b_ref, o_ref):
    c = c_ref[...]
    act = c * jax.nn.sigmoid(c)
    o_ref[...] = jnp.dot(act, w_ref[...], precision=lax.Precision.HIGHEST,
                         preferred_element_type=F32) + b_ref[...]


def _adaln_mod(c, w_ada, b_ada):
    B, D = c.shape
    n_out = w_ada.shape[1]
    return pl.pallas_call(
        _mod_kernel,
        grid=(n_out // D,),
        in_specs=[pl.BlockSpec((B, D), lambda n: (0, 0)),
                  pl.BlockSpec((D, D), lambda n: (0, n)),
                  pl.BlockSpec((1, D), lambda n: (0, n))],
        out_specs=pl.BlockSpec((B, D), lambda n: (0, n)),
        out_shape=jax.ShapeDtypeStruct((B, n_out), F32),
        name="adaln_mod",
    )(c, w_ada, b_ada.reshape(1, n_out))


def _mix_in_kernel(x_ref, mod_ref, g_ref, wqT_ref, wk_ref, wvT_ref, wu_ref, wpool_ref, pscale_ref,
                   cosn_ref, sina_ref, sinb_ref, cost_ref, sint_ref,
                   qT_ref, k_ref, vT_ref, pool_ref, sel_ref,
                   km_ref, tail_ref, *, tm, n_heads, n_blocks):
    i = pl.program_id(1)
    blocks_per_tile = tm // MOBA_BLOCK

    @pl.when(i == 0)
    def _():
        km_ref[...] = jnp.zeros_like(km_ref)
        tail_ref[...] = jnp.zeros_like(tail_ref)

    x = x_ref[0]
    h = (_rms(x) * g_ref[...]) * (1.0 + mod_ref[0, 1:2, :]) + mod_ref[0, 0:1, :]
    hb = h.astype(BF16)

    k = jnp.dot(hb, wk_ref[...], preferred_element_type=F32)
    u = jnp.dot(hb, wu_ref[...], preferred_element_type=F32)
    qT = lax.dot_general(wqT_ref[...], hb, NT_DIMS, preferred_element_type=F32)
    vT = lax.dot_general(wvT_ref[...], hb, NT_DIMS, preferred_element_type=F32)

    cosn, sina, sinb = cosn_ref[...], sina_ref[...], sinb_ref[...]
    k_chunks = []
    for c in range(k.shape[1] // LANES):
        kc = k[:, c * LANES:(c + 1) * LANES]
        kr = (kc * cosn + pltpu.roll(kc, LANES - HALF, axis=1) * sina
              + pltpu.roll(kc, HALF, axis=1) * sinb)
        k_chunks.append(kr)
        k_ref[0, :, c * LANES:(c + 1) * LANES] = kr.astype(BF16)

    for t in range(blocks_per_tile):
        rows = slice(t * MOBA_BLOCK, (t + 1) * MOBA_BLOCK)
        mean = jnp.concatenate([jnp.sum(kr[rows], axis=0, keepdims=True) for kr in k_chunks],
                               axis=1) * (1.0 / MOBA_BLOCK)
        km_ref[pl.ds(i * blocks_per_tile + t, 1), :] = mean

    cos_t, sin_t = cost_ref[...], sint_ref[...]
    q_parts = []
    for hd in range(n_heads):
        x1 = qT[hd * HEAD_DIM:hd * HEAD_DIM + HALF]
        x2 = qT[hd * HEAD_DIM + HALF:(hd + 1) * HEAD_DIM]
        q_parts.append(x1 * cos_t - x2 * sin_t)
        q_parts.append(x2 * cos_t + x1 * sin_t)
    qs = jnp.concatenate(q_parts, axis=0) * (HEAD_DIM ** -0.5)
    qs_hi = qs.astype(BF16)
    qT_ref[0] = qs_hi
    for t in range(blocks_per_tile):
        vT_ref[0, t] = vT[:, t * MOBA_BLOCK:(t + 1) * MOBA_BLOCK].astype(BF16)

    km = km_ref[...]
    km_rep = jnp.concatenate([km] * n_heads, axis=0)
    r_head = lax.broadcasted_iota(jnp.int32, km_rep.shape, 0) // n_blocks
    l_head = lax.broadcasted_iota(jnp.int32, km_rep.shape, 1) // HEAD_DIM
    km_bd = jnp.where(r_head == l_head, km_rep, 0.0)
    km_hi = km_bd.astype(BF16)
    km_lo = (km_bd - km_hi.astype(F32)).astype(BF16)
    qs_lo = (qs - qs_hi.astype(F32)).astype(BF16)
    gate = (jnp.dot(km_hi, qs_hi, preferred_element_type=F32)
            + jnp.dot(km_hi, qs_lo, preferred_element_type=F32)
            + jnp.dot(km_lo, qs_hi, preferred_element_type=F32))

    q_blk = (i * tm + lax.broadcasted_iota(jnp.int32, (n_blocks, tm), 1)) // MOBA_BLOCK
    row_j = lax.broadcasted_iota(jnp.int32, (n_blocks, tm), 0)
    for hd in range(n_heads):
        g = gate[hd * n_blocks:(hd + 1) * n_blocks]
        rank = jnp.zeros((n_blocks, tm), jnp.int32)
        for j in range(n_blocks):
            gj = jnp.broadcast_to(g[j:j + 1, :], g.shape)
            ahead = (gj > g) | ((gj == g) & (j < row_j))
            rank = rank + jnp.where(ahead & (j < q_blk), 1, 0)
        chosen = (row_j < q_blk) & (rank < MOBA_TOPK)
        sel_ref[0, hd * n_blocks:(hd + 1) * n_blocks, :] = jnp.where(chosen, 0.0, -jnp.inf)

    ext = jnp.concatenate([tail_ref[...], u], axis=0)
    tail_ref[...] = u[tm - MAX_WINDOW:, :]
    pos1 = i * tm + lax.broadcasted_iota(jnp.int32, (tm, 1), 0) + 1
    for gi, w in enumerate(POOL_WINDOWS):
        cols = slice(gi * POOL_GROUP, (gi + 1) * POOL_GROUP)
        win = ext[:, cols]
        span = 1
        while span < w:
            win = win + pltpu.roll(win, span, axis=0)
            span *= 2
        count = jnp.minimum(pos1, w).astype(F32)
        d = win[MAX_WINDOW:, :] / count - u[:, cols]
        y = jnp.dot(d.astype(BF16), wpool_ref[gi], preferred_element_type=F32)
        pool_ref[0, :, cols] = (y * pscale_ref[:, cols]).astype(BF16)


def _mix_in(x, mod3, g_pre, wqT, wk, wvT, wu, wpool, pscale, tables, *, tm):
    B, S, D = x.shape
    A = wk.shape[1]
    P = wu.shape[1]
    n_heads = A // HEAD_DIM
    n_blocks = S // MOBA_BLOCK
    bpt = tm // MOBA_BLOCK
    cosn, sina, sinb, cost, sint = tables
    const = lambda *shape: pl.BlockSpec(shape, lambda b, i: (0,) * len(shape),
                                        pipeline_mode=pl.Buffered(1))
    kern = functools.partial(_mix_in_kernel, tm=tm, n_heads=n_heads, n_blocks=n_blocks)
    return pl.pallas_call(
        kern,
        grid=(B, S // tm),
        in_specs=[
            pl.BlockSpec((1, tm, D), lambda b, i: (b, i, 0)),
            pl.BlockSpec((1, 6, D), lambda b, i: (b, 0, 0)),
            const(1, D),
            const(A, D), const(D, A), const(A, D), const(D, P),
            const(len(POOL_WINDOWS), POOL_GROUP, POOL_GROUP),
            const(1, P),
            pl.BlockSpec((tm, LANES), lambda b, i: (i, 0)),
            pl.BlockSpec((tm, LANES), lambda b, i: (i, 0)),
            pl.BlockSpec((tm, LANES), lambda b, i: (i, 0)),
            pl.BlockSpec((HALF, tm), lambda b, i: (0, i)),
            pl.BlockSpec((HALF, tm), lambda b, i: (0, i)),
        ],
        out_specs=[
            pl.BlockSpec((1, A, tm), lambda b, i: (b, 0, i)),
            pl.BlockSpec((1, tm, A), lambda b, i: (b, i, 0)),
            pl.BlockSpec((1, bpt, A, MOBA_BLOCK), lambda b, i: (b, i, 0, 0)),
            pl.BlockSpec((1, tm, P), lambda b, i: (b, i, 0)),
            pl.BlockSpec((1, n_heads * n_blocks, tm), lambda b, i: (b, 0, i)),
        ],
        out_shape=[
            jax.ShapeDtypeStruct((B, A, S), BF16),
            jax.ShapeDtypeStruct((B, S, A), BF16),
            jax.ShapeDtypeStruct((B, n_blocks, A, MOBA_BLOCK), BF16),
            jax.ShapeDtypeStruct((B, S, P), BF16),
            jax.ShapeDtypeStruct((B, n_heads * n_blocks, S), F32),
        ],
        scratch_shapes=[pltpu.VMEM((n_blocks, A), F32), pltpu.VMEM((MAX_WINDOW, P), F32)],
        compiler_params=pltpu.CompilerParams(dimension_semantics=("parallel", "arbitrary"),
                                             vmem_limit_bytes=VMEM_LIMIT),
        name="mix_in",
    )(x, mod3, g_pre, wqT, wk, wvT, wu, wpool, pscale, cosn, sina, sinb, cost, sint)


def _attn_kernel(qT_ref, k_ref, vT_ref, sel_ref, o_ref, qz_ref, m_ref, l_ref, acc_ref,
                 *, n_heads, n_blocks):
    i = pl.program_id(1)
    tq = MOBA_BLOCK
    zeros = jnp.zeros((HEAD_DIM, tq), BF16)
    for hd in range(n_heads):
        qh = qT_ref[0, hd * HEAD_DIM:(hd + 1) * HEAD_DIM, :]
        qz_ref[hd] = jnp.concatenate([qh, zeros] if hd % 2 == 0 else [zeros, qh], axis=0)

    def scores(hd, j):
        c = hd // 2
        kp = k_ref[0, pl.ds(pl.multiple_of(j * MOBA_BLOCK, MOBA_BLOCK), MOBA_BLOCK),
                   c * LANES:(c + 1) * LANES]
        return jnp.dot(kp, qz_ref[hd], preferred_element_type=F32)

    def values(hd, j):
        return vT_ref[0, j, hd * HEAD_DIM:(hd + 1) * HEAD_DIM, :]

    key_i = lax.broadcasted_iota(jnp.int32, (tq, tq), 0)
    qry_i = lax.broadcasted_iota(jnp.int32, (tq, tq), 1)
    for hd in range(n_heads):
        s = jnp.where(key_i <= qry_i, scores(hd, i), -jnp.inf)
        m = jnp.max(s, axis=0, keepdims=True)
        p = jnp.exp(s - m)
        m_ref[hd] = m
        l_ref[hd] = jnp.sum(p, axis=0, keepdims=True)
        acc_ref[hd] = jnp.dot(values(hd, i), p.astype(BF16), preferred_element_type=F32)

    def past_block(j, carry):
        for hd in range(n_heads):
            bias = sel_ref[0, pl.ds(hd * n_blocks + j, 1), :]
            s = scores(hd, j) + bias
            m_old = m_ref[hd]
            m_new = jnp.maximum(m_old, jnp.max(s, axis=0, keepdims=True))
            alpha = jnp.exp(m_old - m_new)
            p = jnp.exp(s - m_new)
            m_ref[hd] = m_new
            l_ref[hd] = alpha * l_ref[hd] + jnp.sum(p, axis=0, keepdims=True)
            acc_ref[hd] = alpha * acc_ref[hd] + jnp.dot(values(hd, j), p.astype(BF16),
                                                        preferred_element_type=F32)
        return carry

    lax.fori_loop(0, i, past_block, 0)

    for c in range(n_heads // 2):
        pair = jnp.concatenate([acc_ref[2 * c] * (1.0 / l_ref[2 * c]),
                                acc_ref[2 * c + 1] * (1.0 / l_ref[2 * c + 1])], axis=0)
        o_ref[0, :, c * LANES:(c + 1) * LANES] = pair.T.astype(BF16)


def _moba_attn(qT, k, vT, sel):
    B, A, S = qT.shape
    n_heads = A // HEAD_DIM
    n_blocks = S // MOBA_BLOCK
    kern = functools.partial(_attn_kernel, n_heads=n_heads, n_blocks=n_blocks)
    return pl.pallas_call(
        kern,
        grid=(B, n_blocks),
        in_specs=[
            pl.BlockSpec((1, A, MOBA_BLOCK), lambda b, i: (b, 0, i)),
            pl.BlockSpec((1, S, A), lambda b, i: (b, 0, 0)),
            pl.BlockSpec((1, n_blocks, A, MOBA_BLOCK), lambda b, i: (b, 0, 0, 0)),
            pl.BlockSpec((1, n_heads * n_blocks, MOBA_BLOCK), lambda b, i: (b, 0, i)),
        ],
        out_specs=pl.BlockSpec((1, MOBA_BLOCK, A), lambda b, i: (b, i, 0)),
        out_shape=jax.ShapeDtypeStruct((B, S, A), BF16),
        scratch_shapes=[
            pltpu.VMEM((n_heads, 2 * HEAD_DIM, MOBA_BLOCK), BF16),
            pltpu.VMEM((n_heads, 1, MOBA_BLOCK), F32),
            pltpu.VMEM((n_heads, 1, MOBA_BLOCK), F32),
            pltpu.VMEM((n_heads, HEAD_DIM, MOBA_BLOCK), F32),
        ],
        compiler_params=pltpu.CompilerParams(dimension_semantics=("parallel", "arbitrary"),
                                             vmem_limit_bytes=VMEM_LIMIT),
        name="moba_attn",
    )(qT, k, vT, sel)


def _out_mlp_kernel(x_ref, attn_ref, pool_ref, mod_ref, g_post_ref, g_pre2_ref, g_post2_ref,
                    wo_ref, wup_ref, wdn_ref, o_ref, *, ff_chunk):
    A = attn_ref.shape[2]
    x = x_ref[0]
    y = (jnp.dot(attn_ref[0], wo_ref[:A, :], preferred_element_type=F32)
         + jnp.dot(pool_ref[0], wo_ref[A:, :], preferred_element_type=F32))
    x1 = x + mod_ref[0, 2:3, :] * (_rms(y) * g_post_ref[...])
    h = (_rms(x1) * g_pre2_ref[...]) * (1.0 + mod_ref[0, 4:5, :]) + mod_ref[0, 3:4, :]
    hb = h.astype(BF16)
    y2 = jnp.zeros_like(x)
    for c in range(wup_ref.shape[1] // ff_chunk):
        cols = slice(c * ff_chunk, (c + 1) * ff_chunk)
        up = jnp.dot(hb, wup_ref[:, cols], preferred_element_type=F32)
        act = jnp.square(jnp.maximum(up, 0.0)).astype(BF16)
        y2 = y2 + jnp.dot(act, wdn_ref[cols, :], preferred_element_type=F32)
    o_ref[0] = x1 + mod_ref[0, 5:6, :] * (_rms(y2) * g_post2_ref[...])


def _out_mlp(x, attn, pool, mod3, g_post, g_pre2, g_post2, wo, wup, wdn, *, tm, ff_chunk):
    B, S, D = x.shape
    A, P, FF = attn.shape[2], pool.shape[2], wup.shape[1]
    const = lambda *shape: pl.BlockSpec(shape, lambda b, i: (0,) * len(shape),
                                        pipeline_mode=pl.Buffered(1))
    return pl.pallas_call(
        functools.partial(_out_mlp_kernel, ff_chunk=ff_chunk),
        grid=(B, S // tm),
        in_specs=[
            pl.BlockSpec((1, tm, D), lambda b, i: (b, i, 0)),
            pl.BlockSpec((1, tm, A), lambda b, i: (b, i, 0)),
            pl.BlockSpec((1, tm, P), lambda b, i: (b, i, 0)),
            pl.BlockSpec((1, 6, D), lambda b, i: (b, 0, 0)),
            const(1, D), const(1, D), const(1, D),
            const(A + P, D), const(D, FF), const(FF, D),
        ],
        out_specs=pl.BlockSpec((1, tm, D), lambda b, i: (b, i, 0)),
        out_shape=jax.ShapeDtypeStruct((B, S, D), F32),
        compiler_params=pltpu.CompilerParams(dimension_semantics=("parallel", "parallel"),
                                             vmem_limit_bytes=VMEM_LIMIT),
        name="out_mlp",
    )(x, attn, pool, mod3, g_post, g_pre2, g_post2, wo, wup, wdn)


def _rope_tables(S):
    inv_freq = 1.0 / (ROPE_THETA ** (jnp.arange(HALF, dtype=F32) * (2.0 / HEAD_DIM)))
    ang = jnp.arange(S, dtype=jnp.int32).astype(F32)[:, None] * inv_freq[None, :]
    cos, sin = jnp.cos(ang), jnp.sin(ang)
    zero = jnp.zeros_like(sin)
    reps = LANES // HEAD_DIM
    cosn = jnp.tile(jnp.concatenate([cos, cos], axis=1), (1, reps))
    sina = jnp.tile(jnp.concatenate([-sin, zero], axis=1), (1, reps))
    sinb = jnp.tile(jnp.concatenate([zero, sin], axis=1), (1, reps))
    return cosn, sina, sinb, cos.T, sin.T


def kernel(x, c, w_ada, b_ada, g_mix_pre, g_mix_post, w_in, w_pool, pool_scale, w_out,
           g_mlp_pre, g_mlp_post, w_up, w_down):
    B, S, D = x.shape
    depth = w_ada.shape[0]
    P = pool_scale.shape[1]
    A = w_out.shape[1] - P
    tables = _rope_tables(S)
    for l in range(depth):
        mod3 = _adaln_mod(c, w_ada[l], b_ada[l]).reshape(B, 6, D)
        w_in_b = w_in[l].astype(BF16)
        wqT = w_in_b[:, :A].T
        wk = w_in_b[:, A:2 * A]
        wvT = w_in_b[:, 2 * A:3 * A].T
        wu = w_in_b[:, 3 * A:]
        qT, k, vT, pool, sel = _mix_in(
            x, mod3, g_mix_pre[l][None, :], wqT, wk, wvT, wu, w_pool[l].astype(BF16),
            pool_scale[l][None, :], tables, tm=512)
        attn = _moba_attn(qT, k, vT, sel)
        x = _out_mlp(x, attn, pool, mod3, g_mix_post[l][None, :], g_mlp_pre[l][None, :],
                     g_mlp_post[l][None, :], w_out[l].astype(BF16), w_up[l].astype(BF16),
                     w_down[l].astype(BF16), tm=512, ff_chunk=1024)
    return x
```

```python
import functools

import jax
import jax.numpy as jnp
from jax import lax
from jax.experimental import pallas as pl
from jax.experimental.pallas import tpu as pltpu

F32 = jnp.float32
BF16 = jnp.bfloat16

HEAD_DIM = 64
HALF = HEAD_DIM // 2
MOBA_BLOCK = 256
MOBA_TOPK = 3
POOL_WINDOWS = (2, 4, 8, 16)
POOL_GROUP = 128
MAX_WINDOW = max(POOL_WINDOWS)
ROPE_THETA = 10000.0
NORM_EPS = 1e-6
Q_SCALE = HEAD_DIM ** -0.5 * 1.4426950408889634
LANES = 128
VMEM_LIMIT = 56 * 1024 * 1024

NT_DIMS = (((1,), (1,)), ((), ()))


def _rms(x):
    return x * lax.rsqrt(jnp.mean(x * x, axis=-1, keepdims=True) + NORM_EPS)


def _mod_kernel(c_ref, w_ref, b_ref, o_ref):
    c = c_ref[...]
    act = c * jax.nn.sigmoid(c)
    o_ref[...] = jnp.dot(act, w_ref[...], precision=lax.Precision.HIGHEST,
                         preferred_element_type=F32) + b_ref[...]


def _adaln_mod(c, w_ada, b_ada):
    B, D = c.shape
    n_out = w_ada.shape[1]
    return pl.pallas_call(
        _mod_kernel,
        grid=(n_out // D,),
        in_specs=[pl.BlockSpec((B, D), lambda n: (0, 0)),
                  pl.BlockSpec((D, D), lambda n: (0, n)),
                  pl.BlockSpec((1, D), lambda n: (0, n))],
        out_specs=pl.BlockSpec((B, D), lambda n: (0, n)),
        out_shape=jax.ShapeDtypeStruct((B, n_out), F32),
        name="adaln_mod",
    )(c, w_ada, b_ada.reshape(1, n_out))


def _mix_in_kernel(x_ref, mod_ref, g_ref, wqT_ref, wk_ref, wvT_ref, wu_ref, wpool_ref, pscale_ref,
                   cosn_ref, sina_ref, sinb_ref, cost_ref, sint_ref,
                   qT_ref, k_ref, vT_ref, pool_ref, sel_ref,
                   km_ref, tail_ref, *, tm, n_heads, n_blocks):
    i = pl.program_id(1)
    blocks_per_tile = tm // MOBA_BLOCK

    @pl.when(i == 0)
    def _():
        km_ref[...] = jnp.zeros_like(km_ref)
        tail_ref[...] = jnp.zeros_like(tail_ref)

    x = x_ref[0]
    h = (_rms(x) * g_ref[...]) * (1.0 + mod_ref[0, 1:2, :]) + mod_ref[0, 0:1, :]
    hb = h.astype(BF16)

    k = jnp.dot(hb, wk_ref[...], preferred_element_type=F32)
    u = jnp.dot(hb, wu_ref[...], preferred_element_type=F32)
    qT = lax.dot_general(wqT_ref[...], hb, NT_DIMS, preferred_element_type=F32)
    vT = lax.dot_general(wvT_ref[...], hb, NT_DIMS, preferred_element_type=F32)

    cosn, sina, sinb = cosn_ref[...], sina_ref[...], sinb_ref[...]
    k_chunks = []
    for c in range(k.shape[1] // LANES):
        kc = k[:, c * LANES:(c + 1) * LANES]
        kr = (kc * cosn + pltpu.roll(kc, LANES - HALF, axis=1) * sina
              + pltpu.roll(kc, HALF, axis=1) * sinb)
        k_chunks.append(kr)
        k_ref[0, :, c * LANES:(c + 1) * LANES] = kr.astype(BF16)

    for t in range(blocks_per_tile):
        rows = slice(t * MOBA_BLOCK, (t + 1) * MOBA_BLOCK)
        mean = jnp.concatenate([jnp.sum(kr[rows], axis=0, keepdims=True) for kr in k_chunks],
                               axis=1) * (1.0 / MOBA_BLOCK)
        km_ref[pl.ds(i * blocks_per_tile + t, 1), :] = mean

    cos_t, sin_t = cost_ref[...], sint_ref[...]
    q_parts = []
    for hd in range(n_heads):
        x1 = qT[hd * HEAD_DIM:hd * HEAD_DIM + HALF]
        x2 = qT[hd * HEAD_DIM + HALF:(hd + 1) * HEAD_DIM]
        q_parts.append(x1 * cos_t - x2 * sin_t)
        q_parts.append(x2 * cos_t + x1 * sin_t)
    qs = jnp.concatenate(q_parts, axis=0) * Q_SCALE
    qs_hi = qs.astype(BF16)
    qT_ref[0] = qs_hi
    for t in range(blocks_per_tile):
        vT_ref[0, t] = vT[:, t * MOBA_BLOCK:(t + 1) * MOBA_BLOCK].astype(BF16)

    km = km_ref[...]
    km_rep = jnp.concatenate([km] * n_heads, axis=0)
    r_head = lax.broadcasted_iota(jnp.int32, km_rep.shape, 0) // n_blocks
    l_head = lax.broadcasted_iota(jnp.int32, km_rep.shape, 1) // HEAD_DIM
    km_bd = jnp.where(r_head == l_head, km_rep, 0.0)
    km_hi = km_bd.astype(BF16)
    km_lo = (km_bd - km_hi.astype(F32)).astype(BF16)
    qs_lo = (qs - qs_hi.astype(F32)).astype(BF16)
    gate = (jnp.dot(km_hi, qs_hi, preferred_element_type=F32)
            + jnp.dot(km_hi, qs_lo, preferred_element_type=F32)
            + jnp.dot(km_lo, qs_hi, preferred_element_type=F32))

    q_blk = (i * tm + lax.broadcasted_iota(jnp.int32, (n_blocks, tm), 1)) // MOBA_BLOCK
    row_j = lax.broadcasted_iota(jnp.int32, (n_blocks, tm), 0)
    for hd in range(n_heads):
        g = gate[hd * n_blocks:(hd + 1) * n_blocks]
        rank = jnp.zeros((n_blocks, tm), jnp.int32)
        for j in range(n_blocks):
            gj = jnp.broadcast_to(g[j:j + 1, :], g.shape)
            ahead = (gj > g) | ((gj == g) & (j < row_j))
            rank = rank + jnp.where(ahead & (j < q_blk), 1, 0)
        chosen = (row_j < q_blk) & (rank < MOBA_TOPK)
        sel_ref[0, hd * n_blocks:(hd + 1) * n_blocks, :] = jnp.where(chosen, 0.0, -jnp.inf)

    ext = jnp.concatenate([tail_ref[...], u], axis=0)
    tail_ref[...] = u[tm - MAX_WINDOW:, :]
    pos1 = i * tm + lax.broadcasted_iota(jnp.int32, (tm, 1), 0) + 1
    for gi, w in enumerate(POOL_WINDOWS):
        cols = slice(gi * POOL_GROUP, (gi + 1) * POOL_GROUP)
        win = ext[:, cols]
        span = 1
        while span < w:
            win = win + pltpu.roll(win, span, axis=0)
            span *= 2
        count = jnp.minimum(pos1, w).astype(F32)
        d = win[MAX_WINDOW:, :] / count - u[:, cols]
        y = jnp.dot(d.astype(BF16), wpool_ref[gi], preferred_element_type=F32)
        pool_ref[0, :, cols] = (y * pscale_ref[:, cols]).astype(BF16)


def _mix_in(x, mod3, g_pre, wqT, wk, wvT, wu, wpool, pscale, tables, *, tm):
    B, S, D = x.shape
    A = wk.shape[1]
    P = wu.shape[1]
    n_heads = A // HEAD_DIM
    n_blocks = S // MOBA_BLOCK
    bpt = tm // MOBA_BLOCK
    cosn, sina, sinb, cost, sint = tables
    const = lambda *shape: pl.BlockSpec(shape, lambda b, i: (0,) * len(shape),
                                        pipeline_mode=pl.Buffered(1))
    kern = functools.partial(_mix_in_kernel, tm=tm, n_heads=n_heads, n_blocks=n_blocks)
    return pl.pallas_call(
        kern,
        grid=(B, S // tm),
        in_specs=[
            pl.BlockSpec((1, tm, D), lambda b, i: (b, i, 0)),
            pl.BlockSpec((1, 6, D), lambda b, i: (b, 0, 0)),
            const(1, D),
            const(A, D), const(D, A), const(A, D), const(D, P),
            const(len(POOL_WINDOWS), POOL_GROUP, POOL_GROUP),
            const(1, P),
            pl.BlockSpec((tm, LANES), lambda b, i: (i, 0)),
            pl.BlockSpec((tm, LANES), lambda b, i: (i, 0)),
            pl.BlockSpec((tm, LANES), lambda b, i: (i, 0)),
            pl.BlockSpec((HALF, tm), lambda b, i: (0, i)),
            pl.BlockSpec((HALF, tm), lambda b, i: (0, i)),
        ],
        out_specs=[
            pl.BlockSpec((1, A, tm), lambda b, i: (b, 0, i)),
            pl.BlockSpec((1, tm, A), lambda b, i: (b, i, 0)),
            pl.BlockSpec((1, bpt, A, MOBA_BLOCK), lambda b, i: (b, i, 0, 0)),
            pl.BlockSpec((1, tm, P), lambda b, i: (b, i, 0)),
            pl.BlockSpec((1, n_heads * n_blocks, tm), lambda b, i: (b, 0, i)),
        ],
        out_shape=[
            jax.ShapeDtypeStruct((B, A, S), BF16),
            jax.ShapeDtypeStruct((B, S, A), BF16),
            jax.ShapeDtypeStruct((B, n_blocks, A, MOBA_BLOCK), BF16),
            jax.ShapeDtypeStruct((B, S, P), BF16),
            jax.ShapeDtypeStruct((B, n_heads * n_blocks, S), F32),
        ],
        scratch_shapes=[pltpu.VMEM((n_blocks, A), F32), pltpu.VMEM((MAX_WINDOW, P), F32)],
        compiler_params=pltpu.CompilerParams(dimension_semantics=("parallel", "arbitrary"),
                                             vmem_limit_bytes=VMEM_LIMIT),
        name="mix_in",
    )(x, mod3, g_pre, wqT, wk, wvT, wu, wpool, pscale, cosn, sina, sinb, cost, sint)


def _attn_kernel(qT_ref, k_ref, vT_ref, sel_ref, o_ref, qz_ref, s_ref, m_ref, l_ref, acc_ref,
                 *, n_heads, n_blocks):
    i = pl.program_id(1)
    tq = MOBA_BLOCK
    zeros = jnp.zeros((HEAD_DIM, tq), BF16)
    for hd in range(n_heads):
        qh = qT_ref[0, hd * HEAD_DIM:(hd + 1) * HEAD_DIM, :]
        qz_ref[hd] = jnp.concatenate([qh, zeros] if hd % 2 == 0 else [zeros, qh], axis=0)

    key_i = lax.broadcasted_iota(jnp.int32, (tq, tq), 0)
    qry_i = lax.broadcasted_iota(jnp.int32, (tq, tq), 1)

    def block_update(j, own):
        alphas, shifts = [], []
        for hd in range(n_heads):
            c = hd // 2
            kp = k_ref[0, pl.ds(pl.multiple_of(j * MOBA_BLOCK, MOBA_BLOCK), MOBA_BLOCK),
                       c * LANES:(c + 1) * LANES]
            s = jnp.dot(kp, qz_ref[hd], preferred_element_type=F32)
            if own:
                s = jnp.where(key_i <= qry_i, s, -jnp.inf)
            s_ref[hd] = s
            cmax = jnp.max(s, axis=0, keepdims=True)
            if own:
                m_ref[hd] = cmax
                alphas.append(None)
                shifts.append(cmax)
            else:
                bias = sel_ref[0, pl.ds(hd * n_blocks + j, 1), :]
                m_old = m_ref[hd]
                m_new = jnp.maximum(m_old, cmax + bias)
                m_ref[hd] = m_new
                alphas.append(jnp.exp2(m_old - m_new))
                shifts.append(m_new - bias)
        for hd in range(n_heads):
            p = jnp.exp2(s_ref[hd] - shifts[hd])
            psum = jnp.sum(p, axis=0, keepdims=True)
            pv = jnp.dot(vT_ref[0, j, hd * HEAD_DIM:(hd + 1) * HEAD_DIM, :], p.astype(BF16),
                         preferred_element_type=F32)
            if own:
                l_ref[hd] = psum
                acc_ref[hd] = pv
            else:
                l_ref[hd] = alphas[hd] * l_ref[hd] + psum
                acc_ref[hd] = alphas[hd] * acc_ref[hd] + pv

    block_update(i, True)

    def past_block(j, carry):
        block_update(j, False)
        return carry

    lax.fori_loop(0, i, past_block, 0)

    for c in range(n_heads // 2):
        pair = jnp.concatenate([acc_ref[2 * c] * (1.0 / l_ref[2 * c]),
                                acc_ref[2 * c + 1] * (1.0 / l_ref[2 * c + 1])], axis=0)
        o_ref[0, :, c * LANES:(c + 1) * LANES] = pair.T.astype(BF16)


def _moba_attn(qT, k, vT, sel):
    B, A, S = qT.shape
    n_heads = A // HEAD_DIM
    n_blocks = S // MOBA_BLOCK
    kern = functools.partial(_attn_kernel, n_heads=n_heads, n_blocks=n_blocks)
    return pl.pallas_call(
        kern,
        grid=(B, n_blocks),
        in_specs=[
            pl.BlockSpec((1, A, MOBA_BLOCK), lambda b, i: (b, 0, i)),
            pl.BlockSpec((1, S, A), lambda b, i: (b, 0, 0)),
            pl.BlockSpec((1, n_blocks, A, MOBA_BLOCK), lambda b, i: (b, 0, 0, 0)),
            pl.BlockSpec((1, n_heads * n_blocks, MOBA_BLOCK), lambda b, i: (b, 0, i)),
        ],
        out_specs=pl.BlockSpec((1, MOBA_BLOCK, A), lambda b, i: (b, i, 0)),
        out_shape=jax.ShapeDtypeStruct((B, S, A), BF16),
        scratch_shapes=[
            pltpu.VMEM((n_heads, 2 * HEAD_DIM, MOBA_BLOCK), BF16),
            pltpu.VMEM((n_heads, MOBA_BLOCK, MOBA_BLOCK), F32),
            pltpu.VMEM((n_heads, 1, MOBA_BLOCK), F32),
            pltpu.VMEM((n_heads, 1, MOBA_BLOCK), F32),
            pltpu.VMEM((n_heads, HEAD_DIM, MOBA_BLOCK), F32),
        ],
        compiler_params=pltpu.CompilerParams(dimension_semantics=("parallel", "arbitrary"),
                                             vmem_limit_bytes=VMEM_LIMIT),
        name="moba_attn",
    )(qT, k, vT, sel)


def _out_mlp_kernel(x_ref, attn_ref, pool_ref, mod_ref, g_post_ref, g_pre2_ref, g_post2_ref,
                    wo_ref, wup_ref, wdn_ref, o_ref, *, ff_chunk):
    A = attn_ref.shape[2]
    x = x_ref[0]
    y = (jnp.dot(attn_ref[0], wo_ref[:A, :], preferred_element_type=F32)
         + jnp.dot(pool_ref[0], wo_ref[A:, :], preferred_element_type=F32))
    x1 = x + mod_ref[0, 2:3, :] * (_rms(y) * g_post_ref[...])
    h = (_rms(x1) * g_pre2_ref[...]) * (1.0 + mod_ref[0, 4:5, :]) + mod_ref[0, 3:4, :]
    hb = h.astype(BF16)
    y2 = jnp.zeros_like(x)
    for c in range(wup_ref.shape[1] // ff_chunk):
        cols = slice(c * ff_chunk, (c + 1) * ff_chunk)
        up = jnp.dot(hb, wup_ref[:, cols], preferred_element_type=F32)
        act = jnp.square(jnp.maximum(up, 0.0)).astype(BF16)
        y2 = y2 + jnp.dot(act, wdn_ref[cols, :], preferred_element_type=F32)
    o_ref[0] = x1 + mod_ref[0, 5:6, :] * (_rms(y2) * g_post2_ref[...])


def _out_mlp(x, attn, pool, mod3, g_post, g_pre2, g_post2, wo, wup, wdn, *, tm, ff_chunk):
    B, S, D = x.shape
    A, P, FF = attn.shape[2], pool.shape[2], wup.shape[1]
    const = lambda *shape: pl.BlockSpec(shape, lambda b, i: (0,) * len(shape),
                                        pipeline_mode=pl.Buffered(1))
    return pl.pallas_call(
        functools.partial(_out_mlp_kernel, ff_chunk=ff_chunk),
        grid=(B, S // tm),
        in_specs=[
            pl.BlockSpec((1, tm, D), lambda b, i: (b, i, 0)),
            pl.BlockSpec((1, tm, A), lambda b, i: (b, i, 0)),
            pl.BlockSpec((1, tm, P), lambda b, i: (b, i, 0)),
            pl.BlockSpec((1, 6, D), lambda b, i: (b, 0, 0)),
            const(1, D), const(1, D), const(1, D),
            const(A + P, D), const(D, FF), const(FF, D),
        ],
        out_specs=pl.BlockSpec((1, tm, D), lambda b, i: (b, i, 0)),
        out_shape=jax.ShapeDtypeStruct((B, S, D), F32),
        compiler_params=pltpu.CompilerParams(dimension_semantics=("parallel", "parallel"),
                                             vmem_limit_bytes=VMEM_LIMIT),
        name="out_mlp",
    )(x, attn, pool, mod3, g_post, g_pre2, g_post2, wo, wup, wdn)


def _rope_tables(S):
    inv_freq = 1.0 / (ROPE_THETA ** (jnp.arange(HALF, dtype=F32) * (2.0 / HEAD_DIM)))
    ang = jnp.arange(S, dtype=jnp.int32).astype(F32)[:, None] * inv_freq[None, :]
    cos, sin = jnp.cos(ang), jnp.sin(ang)
    zero = jnp.zeros_like(sin)
    reps = LANES // HEAD_DIM
    cosn = jnp.tile(jnp.concatenate([cos, cos], axis=1), (1, reps))
    sina = jnp.tile(jnp.concatenate([-sin, zero], axis=1), (1, reps))
    sinb = jnp.tile(jnp.concatenate([zero, sin], axis=1), (1, reps))
    return cosn, sina, sinb, cos.T, sin.T


def kernel(x, c, w_ada, b_ada, g_mix_pre, g_mix_post, w_in, w_pool, pool_scale, w_out,
           g_mlp_pre, g_mlp_post, w_up, w_down):
    B, S, D = x.shape
    depth = w_ada.shape[0]
    P = pool_scale.shape[1]
    A = w_out.shape[1] - P
    tables = _rope_tables(S)
    for l in range(depth):
        mod3 = _adaln_mod(c, w_ada[l], b_ada[l]).reshape(B, 6, D)
        w_in_b = w_in[l].astype(BF16)
        wqT = w_in_b[:, :A].T
        wk = w_in_b[:, A:2 * A]
        wvT = w_in_b[:, 2 * A:3 * A].T
        wu = w_in_b[:, 3 * A:]
        qT, k, vT, pool, sel = _mix_in(
            x, mod3, g_mix_pre[l][None, :], wqT, wk, wvT, wu, w_pool[l].astype(BF16),
            pool_scale[l][None, :], tables, tm=512)
        attn = _moba_attn(qT, k, vT, sel)
        x = _out_mlp(x, attn, pool, mod3, g_mix_post[l][None, :], g_mlp_pre[l][None, :],
                     g_mlp_post[l][None, :], w_out[l].astype(BF16), w_up[l].astype(BF16),
                     w_down[l].astype(BF16), tm=512, ff_chunk=1024)
    return x
```

```python
import functools

import jax
import jax.numpy as jnp
from jax import lax
from jax.experimental import pallas as pl
from jax.experimental.pallas import tpu as pltpu

F32 = jnp.float32
BF16 = jnp.bfloat16

HEAD_DIM = 64
HALF = HEAD_DIM // 2
MOBA_BLOCK = 256
MOBA_TOPK = 3
POOL_WINDOWS = (2, 4, 8, 16)
POOL_GROUP = 128
MAX_WINDOW = max(POOL_WINDOWS)
ROPE_THETA = 10000.0
NORM_EPS = 1e-6
Q_SCALE = HEAD_DIM ** -0.5 * 1.4426950408889634
BF16_ROWS = 16
V_ROWS = HEAD_DIM + BF16_ROWS
LANES = 128
PAIR = LANES // HEAD_DIM
VMEM_LIMIT = 56 * 1024 * 1024

NT_DIMS = (((1,), (1,)), ((), ()))


def _rms(x):
    return x * lax.rsqrt(jnp.mean(x * x, axis=-1, keepdims=True) + NORM_EPS)


def _mod_kernel(c_ref, w_ref, b_ref, o_ref):
    c = c_ref[...]
    act = c * jax.nn.sigmoid(c)
    o_ref[...] = jnp.dot(act, w_ref[...], precision=lax.Precision.HIGHEST,
                         preferred_element_type=F32) + b_ref[...]


def _adaln_mod(c, w_ada, b_ada):
    B, D = c.shape
    n_out = w_ada.shape[1]
    return pl.pallas_call(
        _mod_kernel,
        grid=(n_out // D,),
        in_specs=[pl.BlockSpec((B, D), lambda n: (0, 0)),
                  pl.BlockSpec((D, D), lambda n: (0, n)),
                  pl.BlockSpec((1, D), lambda n: (0, n))],
        out_specs=pl.BlockSpec((B, D), lambda n: (0, n)),
        out_shape=jax.ShapeDtypeStruct((B, n_out), F32),
        name="adaln_mod",
    )(c, w_ada, b_ada.reshape(1, n_out))


def _mix_in_kernel(x_ref, mod_ref, g_ref, wqT_ref, wk_ref, wvT_ref, wu_ref, wpool_ref, pscale_ref,
                   cosn_ref, sina_ref, sinb_ref, cost_ref, sint_ref,
                   qT_ref, k_ref, vT_ref, pool_ref, sel_ref,
                   km_ref, tail_ref, *, tm, n_heads, n_blocks):
    i = pl.program_id(1)
    blocks_per_tile = tm // MOBA_BLOCK

    @pl.when(i == 0)
    def _():
        km_ref[...] = jnp.zeros_like(km_ref)
        tail_ref[...] = jnp.zeros_like(tail_ref)

    x = x_ref[0]
    h = (_rms(x) * g_ref[...]) * (1.0 + mod_ref[0, 1:2, :]) + mod_ref[0, 0:1, :]
    hb = h.astype(BF16)

    k = jnp.dot(hb, wk_ref[...], preferred_element_type=F32)
    u = jnp.dot(hb, wu_ref[...], preferred_element_type=F32)
    qT = lax.dot_general(wqT_ref[...], hb, NT_DIMS, preferred_element_type=F32)
    vT = lax.dot_general(wvT_ref[...], hb, NT_DIMS, preferred_element_type=F32)

    cosn, sina, sinb = cosn_ref[...], sina_ref[...], sinb_ref[...]
    k_chunks = []
    for c in range(k.shape[1] // LANES):
        kc = k[:, c * LANES:(c + 1) * LANES]
        kr = (kc * cosn + pltpu.roll(kc, LANES - HALF, axis=1) * sina
              + pltpu.roll(kc, HALF, axis=1) * sinb)
        k_chunks.append(kr)
        k_ref[0, :, c * LANES:(c + 1) * LANES] = kr.astype(BF16)

    for t in range(blocks_per_tile):
        rows = slice(t * MOBA_BLOCK, (t + 1) * MOBA_BLOCK)
        mean = jnp.concatenate([jnp.sum(kr[rows], axis=0, keepdims=True) for kr in k_chunks],
                               axis=1) * (1.0 / MOBA_BLOCK)
        km_ref[pl.ds(i * blocks_per_tile + t, 1), :] = mean

    cos_t, sin_t = cost_ref[...], sint_ref[...]
    q_parts = []
    for hd in range(n_heads):
        x1 = qT[hd * HEAD_DIM:hd * HEAD_DIM + HALF]
        x2 = qT[hd * HEAD_DIM + HALF:(hd + 1) * HEAD_DIM]
        q_parts.append(x1 * cos_t - x2 * sin_t)
        q_parts.append(x2 * cos_t + x1 * sin_t)
    qs = jnp.concatenate(q_parts, axis=0) * Q_SCALE
    qs_hi = qs.astype(BF16)
    qT_ref[0] = qs_hi
    ones_rows = jnp.where(lax.broadcasted_iota(jnp.int32, (V_ROWS - HEAD_DIM, tm), 0) == 0, 1.0, 0.0)
    v_parts = []
    for hd in range(n_heads):
        v_parts += [vT[hd * HEAD_DIM:(hd + 1) * HEAD_DIM], ones_rows]
    v_aug = jnp.concatenate(v_parts, axis=0).astype(BF16)
    for t in range(blocks_per_tile):
        vT_ref[0, t] = v_aug[:, t * MOBA_BLOCK:(t + 1) * MOBA_BLOCK]

    km = km_ref[...]
    km_rep = jnp.concatenate([km] * n_heads, axis=0)
    r_head = lax.broadcasted_iota(jnp.int32, km_rep.shape, 0) // n_blocks
    l_head = lax.broadcasted_iota(jnp.int32, km_rep.shape, 1) // HEAD_DIM
    km_bd = jnp.where(r_head == l_head, km_rep, 0.0)
    km_hi = km_bd.astype(BF16)
    km_lo = (km_bd - km_hi.astype(F32)).astype(BF16)
    qs_lo = (qs - qs_hi.astype(F32)).astype(BF16)
    gate = (jnp.dot(km_hi, qs_hi, preferred_element_type=F32)
            + jnp.dot(km_hi, qs_lo, preferred_element_type=F32)
            + jnp.dot(km_lo, qs_hi, preferred_element_type=F32))

    q_blk = (i * tm + lax.broadcasted_iota(jnp.int32, (n_blocks, tm), 1)) // MOBA_BLOCK
    row_j = lax.broadcasted_iota(jnp.int32, (n_blocks, tm), 0)
    for hd in range(n_heads):
        g = gate[hd * n_blocks:(hd + 1) * n_blocks]
        rank = jnp.zeros((n_blocks, tm), jnp.int32)
        for j in range(n_blocks):
            gj = jnp.broadcast_to(g[j:j + 1, :], g.shape)
            ahead = (gj > g) | ((gj == g) & (j < row_j))
            rank = rank + jnp.where(ahead & (j < q_blk), 1, 0)
        chosen = (row_j < q_blk) & (rank < MOBA_TOPK)
        sel_ref[0, hd * n_blocks:(hd + 1) * n_blocks, :] = jnp.where(chosen, 0.0, -jnp.inf)

    ext = jnp.concatenate([tail_ref[...], u], axis=0)
    tail_ref[...] = u[tm - MAX_WINDOW:, :]
    pos1 = i * tm + lax.broadcasted_iota(jnp.int32, (tm, 1), 0) + 1
    for gi, w in enumerate(POOL_WINDOWS):
        cols = slice(gi * POOL_GROUP, (gi + 1) * POOL_GROUP)
        win = ext[:, cols]
        span = 1
        while span < w:
            win = win + pltpu.roll(win, span, axis=0)
            span *= 2
        count = jnp.minimum(pos1, w).astype(F32)
        d = win[MAX_WINDOW:, :] / count - u[:, cols]
        y = jnp.dot(d.astype(BF16), wpool_ref[gi], preferred_element_type=F32)
        pool_ref[0, :, cols] = (y * pscale_ref[:, cols]).astype(BF16)


def _mix_in(x, mod3, g_pre, wqT, wk, wvT, wu, wpool, pscale, tables, *, tm):
    B, S, D = x.shape
    A = wk.shape[1]
    P = wu.shape[1]
    n_heads = A // HEAD_DIM
    n_blocks = S // MOBA_BLOCK
    bpt = tm // MOBA_BLOCK
    cosn, sina, sinb, cost, sint = tables
    const = lambda *shape: pl.BlockSpec(shape, lambda b, i: (0,) * len(shape),
                                        pipeline_mode=pl.Buffered(1))
    kern = functools.partial(_mix_in_kernel, tm=tm, n_heads=n_heads, n_blocks=n_blocks)
    return pl.pallas_call(
        kern,
        grid=(B, S // tm),
        in_specs=[
            pl.BlockSpec((1, tm, D), lambda b, i: (b, i, 0)),
            pl.BlockSpec((1, 6, D), lambda b, i: (b, 0, 0)),
            const(1, D),
            const(A, D), const(D, A), const(A, D), const(D, P),
            const(len(POOL_WINDOWS), POOL_GROUP, POOL_GROUP),
            const(1, P),
            pl.BlockSpec((tm, LANES), lambda b, i: (i, 0)),
            pl.BlockSpec((tm, LANES), lambda b, i: (i, 0)),
            pl.BlockSpec((tm, LANES), lambda b, i: (i, 0)),
            pl.BlockSpec((HALF, tm), lambda b, i: (0, i)),
            pl.BlockSpec((HALF, tm), lambda b, i: (0, i)),
        ],
        out_specs=[
            pl.BlockSpec((1, A, tm), lambda b, i: (b, 0, i)),
            pl.BlockSpec((1, tm, A), lambda b, i: (b, i, 0)),
            pl.BlockSpec((1, bpt, n_heads * V_ROWS, MOBA_BLOCK), lambda b, i: (b, i, 0, 0)),
            pl.BlockSpec((1, tm, P), lambda b, i: (b, i, 0)),
            pl.BlockSpec((1, n_heads * n_blocks, tm), lambda b, i: (b, 0, i)),
        ],
        out_shape=[
            jax.ShapeDtypeStruct((B, A, S), BF16),
            jax.ShapeDtypeStruct((B, S, A), BF16),
            jax.ShapeDtypeStruct((B, n_blocks, n_heads * V_ROWS, MOBA_BLOCK), BF16),
            jax.ShapeDtypeStruct((B, S, P), BF16),
            jax.ShapeDtypeStruct((B, n_heads * n_blocks, S), F32),
        ],
        scratch_shapes=[pltpu.VMEM((n_blocks, A), F32), pltpu.VMEM((MAX_WINDOW, P), F32)],
        compiler_params=pltpu.CompilerParams(dimension_semantics=("parallel", "arbitrary"),
                                             vmem_limit_bytes=VMEM_LIMIT),
        name="mix_in",
    )(x, mod3, g_pre, wqT, wk, wvT, wu, wpool, pscale, cosn, sina, sinb, cost, sint)


def _attn_kernel(qT_ref, k_ref, vT_ref, sel_ref, o_ref, qz_ref, m_ref, acc_ref,
                 s0_ref, alpha0_ref, shift0_ref, s1_ref, alpha1_ref, shift1_ref,
                 *, rows, n_heads, n_blocks):
    i = pl.program_id(1)
    tq = MOBA_BLOCK
    zeros = jnp.zeros((HEAD_DIM, tq), BF16)
    items = [(r, hd) for r in range(rows) for hd in range(n_heads)]
    for it, (r, hd) in enumerate(items):
        qh = qT_ref[r, hd * HEAD_DIM:(hd + 1) * HEAD_DIM, :]
        qz_ref[it] = jnp.concatenate([qh, zeros] if hd % PAIR == 0 else [zeros, qh], axis=0)
    acc_ref[...] = jnp.zeros_like(acc_ref)
    buffers = ((s0_ref, alpha0_ref, shift0_ref), (s1_ref, alpha1_ref, shift1_ref))

    def pass1(j, own, dst):
        s_ref, alpha_ref, shift_ref = dst
        keys = pl.ds(pl.multiple_of(j * tq, tq), tq)
        for it, (r, hd) in enumerate(items):
            c = hd // PAIR
            s = jnp.dot(k_ref[r, keys, c * LANES:(c + 1) * LANES], qz_ref[it],
                        preferred_element_type=F32)
            if own:
                key_i = lax.broadcasted_iota(jnp.int32, (tq, tq), 0)
                qry_i = lax.broadcasted_iota(jnp.int32, (tq, tq), 1)
                s = jnp.where(key_i <= qry_i, s, -jnp.inf)
            s_ref[it] = s
            cmax = jnp.max(s, axis=0, keepdims=True)
            if own:
                m_ref[it] = cmax
                alpha_ref[it] = jnp.zeros_like(cmax)
                shift_ref[it] = cmax
            else:
                bias = sel_ref[r, pl.ds(hd * n_blocks + j, 1), :]
                m_old = m_ref[it]
                m_new = jnp.maximum(m_old, cmax + bias)
                m_ref[it] = m_new
                alpha_ref[it] = jnp.exp2(m_old - m_new)
                shift_ref[it] = m_new - bias

    def pass2(j, src):
        s_ref, alpha_ref, shift_ref = src
        for it, (r, hd) in enumerate(items):
            p = jnp.exp2((s_ref[it] - shift_ref[it]).astype(BF16))
            pv = jnp.dot(vT_ref[r, j, hd * V_ROWS:(hd + 1) * V_ROWS, :], p,
                         preferred_element_type=F32)
            acc_ref[it] = alpha_ref[it] * acc_ref[it] + pv

    def nth_block(n):
        return jnp.where(n == 0, i, n - 1)

    pass1(i, True, buffers[0])

    def two_turns(u, carry):
        t = 2 * u
        pass2(nth_block(t), buffers[0])
        pass1(t, False, buffers[1])
        pass1(t + 1, False, buffers[0])
        pass2(t, buffers[1])
        return carry

    lax.fori_loop(0, i // 2, two_turns, 0)

    @pl.when(i % 2 == 1)
    def _():
        pass1(i - 1, False, buffers[1])
        pass2(nth_block(i - 1), buffers[0])
        pass2(i - 1, buffers[1])

    @pl.when(i % 2 == 0)
    def _():
        pass2(nth_block(i), buffers[0])

    for r in range(rows):
        for c in range(n_heads // PAIR):
            pair = [acc_ref[it, :HEAD_DIM] * (1.0 / acc_ref[it, HEAD_DIM:HEAD_DIM + 1])
                    for it in range(r * n_heads + c * PAIR, r * n_heads + (c + 1) * PAIR)]
            o_ref[r, :, c * LANES:(c + 1) * LANES] = jnp.concatenate(pair, axis=0).T.astype(BF16)


def _moba_attn(qT, k, vT, sel, *, rows):
    B, A, S = qT.shape
    n_heads = A // HEAD_DIM
    n_blocks = S // MOBA_BLOCK
    n_items = rows * n_heads
    kern = functools.partial(_attn_kernel, rows=rows, n_heads=n_heads, n_blocks=n_blocks)
    stage = [pltpu.VMEM((n_items, MOBA_BLOCK, MOBA_BLOCK), F32),
             pltpu.VMEM((n_items, 1, MOBA_BLOCK), F32),
             pltpu.VMEM((n_items, 1, MOBA_BLOCK), F32)]
    return pl.pallas_call(
        kern,
        grid=(B // rows, n_blocks),
        in_specs=[
            pl.BlockSpec((rows, A, MOBA_BLOCK), lambda b, i: (b, 0, i)),
            pl.BlockSpec((rows, S, A), lambda b, i: (b, 0, 0)),
            pl.BlockSpec((rows, n_blocks, n_heads * V_ROWS, MOBA_BLOCK), lambda b, i: (b, 0, 0, 0)),
            pl.BlockSpec((rows, n_heads * n_blocks, MOBA_BLOCK), lambda b, i: (b, 0, i)),
        ],
        out_specs=pl.BlockSpec((rows, MOBA_BLOCK, A), lambda b, i: (b, i, 0)),
        out_shape=jax.ShapeDtypeStruct((B, S, A), BF16),
        scratch_shapes=[
            pltpu.VMEM((n_items, LANES, MOBA_BLOCK), BF16),
            pltpu.VMEM((n_items, 1, MOBA_BLOCK), F32),
            pltpu.VMEM((n_items, V_ROWS, MOBA_BLOCK), F32),
        ] + stage + stage,
        compiler_params=pltpu.CompilerParams(dimension_semantics=("parallel", "arbitrary"),
                                             vmem_limit_bytes=VMEM_LIMIT),
        name="moba_attn",
    )(qT, k, vT, sel)


def _out_mlp_kernel(x_ref, attn_ref, pool_ref, mod_ref, g_post_ref, g_pre2_ref, g_post2_ref,
                    wo_ref, wup_ref, wdn_ref, o_ref, *, ff_chunk):
    A = attn_ref.shape[2]
    x = x_ref[0]
    y = (jnp.dot(attn_ref[0], wo_ref[:A, :], preferred_element_type=F32)
         + jnp.dot(pool_ref[0], wo_ref[A:, :], preferred_element_type=F32))
    x1 = x + mod_ref[0, 2:3, :] * (_rms(y) * g_post_ref[...])
    h = (_rms(x1) * g_pre2_ref[...]) * (1.0 + mod_ref[0, 4:5, :]) + mod_ref[0, 3:4, :]
    hb = h.astype(BF16)
    y2 = jnp.zeros_like(x)
    for c in range(wup_ref.shape[1] // ff_chunk):
        cols = slice(c * ff_chunk, (c + 1) * ff_chunk)
        up = jnp.dot(hb, wup_ref[:, cols], preferred_element_type=F32)
        act = jnp.square(jnp.maximum(up, 0.0)).astype(BF16)
        y2 = y2 + jnp.dot(act, wdn_ref[cols, :], preferred_element_type=F32)
    o_ref[0] = x1 + mod_ref[0, 5:6, :] * (_rms(y2) * g_post2_ref[...])


def _out_mlp(x, attn, pool, mod3, g_post, g_pre2, g_post2, wo, wup, wdn, *, tm, ff_chunk):
    B, S, D = x.shape
    A, P, FF = attn.shape[2], pool.shape[2], wup.shape[1]
    const = lambda *shape: pl.BlockSpec(shape, lambda b, i: (0,) * len(shape),
                                        pipeline_mode=pl.Buffered(1))
    return pl.pallas_call(
        functools.partial(_out_mlp_kernel, ff_chunk=ff_chunk),
        grid=(B, S // tm),
        in_specs=[
            pl.BlockSpec((1, tm, D), lambda b, i: (b, i, 0)),
            pl.BlockSpec((1, tm, A), lambda b, i: (b, i, 0)),
            pl.BlockSpec((1, tm, P), lambda b, i: (b, i, 0)),
            pl.BlockSpec((1, 6, D), lambda b, i: (b, 0, 0)),
            const(1, D), const(1, D), const(1, D),
            const(A + P, D), const(D, FF), const(FF, D),
        ],
        out_specs=pl.BlockSpec((1, tm, D), lambda b, i: (b, i, 0)),
        out_shape=jax.ShapeDtypeStruct((B, S, D), F32),
        compiler_params=pltpu.CompilerParams(dimension_semantics=("parallel", "parallel"),
                                             vmem_limit_bytes=VMEM_LIMIT),
        name="out_mlp",
    )(x, attn, pool, mod3, g_post, g_pre2, g_post2, wo, wup, wdn)


def _rope_tables(S):
    inv_freq = 1.0 / (ROPE_THETA ** (jnp.arange(HALF, dtype=F32) * (2.0 / HEAD_DIM)))
    ang = jnp.arange(S, dtype=jnp.int32).astype(F32)[:, None] * inv_freq[None, :]
    cos, sin = jnp.cos(ang), jnp.sin(ang)
    zero = jnp.zeros_like(sin)
    reps = LANES // HEAD_DIM
    cosn = jnp.tile(jnp.concatenate([cos, cos], axis=1), (1, reps))
    sina = jnp.tile(jnp.concatenate([-sin, zero], axis=1), (1, reps))
    sinb = jnp.tile(jnp.concatenate([zero, sin], axis=1), (1, reps))
    return cosn, sina, sinb, cos.T, sin.T


def kernel(x, c, w_ada, b_ada, g_mix_pre, g_mix_post, w_in, w_pool, pool_scale, w_out,
           g_mlp_pre, g_mlp_post, w_up, w_down):
    B, S, D = x.shape
    depth = w_ada.shape[0]
    P = pool_scale.shape[1]
    A = w_out.shape[1] - P
    tables = _rope_tables(S)
    for l in range(depth):
        mod3 = _adaln_mod(c, w_ada[l], b_ada[l]).reshape(B, 6, D)
        w_in_b = w_in[l].astype(BF16)
        wqT = w_in_b[:, :A].T
        wk = w_in_b[:, A:2 * A]
        wvT = w_in_b[:, 2 * A:3 * A].T
        wu = w_in_b[:, 3 * A:]
        qT, k, vT, pool, sel = _mix_in(
            x, mod3, g_mix_pre[l][None, :], wqT, wk, wvT, wu, w_pool[l].astype(BF16),
            pool_scale[l][None, :], tables, tm=512)
        attn = _moba_attn(qT, k, vT, sel, rows=2 if B % 2 == 0 else 1)
        x = _out_mlp(x, attn, pool, mod3, g_mix_post[l][None, :], g_mlp_pre[l][None, :],
                     g_mlp_post[l][None, :], w_out[l].astype(BF16), w_up[l].astype(BF16),
                     w_down[l].astype(BF16), tm=512, ff_chunk=1024)
    return x
```

```python
import functools

import jax
import jax.numpy as jnp
from jax import lax
from jax.experimental import pallas as pl
from jax.experimental.pallas import tpu as pltpu

F32 = jnp.float32
BF16 = jnp.bfloat16

HEAD_DIM = 64
HALF = HEAD_DIM // 2
MOBA_BLOCK = 256
MOBA_TOPK = 3
POOL_WINDOWS = (2, 4, 8, 16)
POOL_GROUP = 128
MAX_WINDOW = max(POOL_WINDOWS)
ROPE_THETA = 10000.0
NORM_EPS = 1e-6
Q_SCALE = HEAD_DIM ** -0.5 * 1.4426950408889634
BF16_ROWS = 16
V_ROWS = HEAD_DIM + BF16_ROWS
LANES = 128
PAIR = LANES // HEAD_DIM
VMEM_LIMIT = 56 * 1024 * 1024

NT_DIMS = (((1,), (1,)), ((), ()))


def _rms(x):
    return x * lax.rsqrt(jnp.mean(x * x, axis=-1, keepdims=True) + NORM_EPS)


def _mod_kernel(c_ref, w_ref, b_ref, o_ref):
    c = c_ref[...]
    act = c * jax.nn.sigmoid(c)
    o_ref[...] = jnp.dot(act, w_ref[...], precision=lax.Precision.HIGHEST,
                         preferred_element_type=F32) + b_ref[...]


def _adaln_mod(c, w_ada, b_ada):
    B, D = c.shape
    n_out = w_ada.shape[1]
    return pl.pallas_call(
        _mod_kernel,
        grid=(n_out // D,),
        in_specs=[pl.BlockSpec((B, D), lambda n: (0, 0)),
                  pl.BlockSpec((D, D), lambda n: (0, n)),
                  pl.BlockSpec((1, D), lambda n: (0, n))],
        out_specs=pl.BlockSpec((B, D), lambda n: (0, n)),
        out_shape=jax.ShapeDtypeStruct((B, n_out), F32),
        name="adaln_mod",
    )(c, w_ada, b_ada.reshape(1, n_out))


def _mix_in_kernel(x_ref, mod_ref, g_ref, wqT_ref, wk_ref, wvT_ref, wu_ref, wpool_ref, pscale_ref,
                   cosn_ref, sina_ref, sinb_ref, cost_ref, sint_ref,
                   qT_ref, k_ref, vT_ref, pool_ref, sel_ref,
                   km_ref, tail_ref, *, tm, sub_rows, n_heads, n_blocks):
    i = pl.program_id(1)
    blocks_per_group = sub_rows // MOBA_BLOCK

    @pl.when(i == 0)
    def _():
        km_ref[...] = jnp.zeros_like(km_ref)
        tail_ref[...] = jnp.zeros_like(tail_ref)

    groups = [slice(r * sub_rows, (r + 1) * sub_rows) for r in range(tm // sub_rows)]
    hbs = []
    for rows in groups:
        x = x_ref[0, rows, :]
        h = (_rms(x) * g_ref[...]) * (1.0 + mod_ref[0, 1:2, :]) + mod_ref[0, 0:1, :]
        hbs.append(h.astype(BF16))
    for gr, (rows, hb) in enumerate(zip(groups, hbs)):
        _mix_in_group(i * tm + gr * sub_rows, i * (tm // MOBA_BLOCK) + gr * blocks_per_group,
                      rows, gr * blocks_per_group, hb,
                      wqT_ref, wk_ref, wvT_ref, wu_ref, wpool_ref, pscale_ref,
                      cosn_ref, sina_ref, sinb_ref, cost_ref, sint_ref,
                      qT_ref, k_ref, vT_ref, pool_ref, sel_ref, km_ref, tail_ref,
                      n_heads=n_heads, n_blocks=n_blocks)


def _mix_in_group(pos0, blk0, rows, tile_blk0, hb,
                  wqT_ref, wk_ref, wvT_ref, wu_ref, wpool_ref, pscale_ref,
                  cosn_ref, sina_ref, sinb_ref, cost_ref, sint_ref,
                  qT_ref, k_ref, vT_ref, pool_ref, sel_ref, km_ref, tail_ref, *, n_heads, n_blocks):
    tg = hb.shape[0]
    blocks_per_group = tg // MOBA_BLOCK
    k = jnp.dot(hb, wk_ref[...], preferred_element_type=F32)
    u = jnp.dot(hb, wu_ref[...], preferred_element_type=F32)
    qT = lax.dot_general(wqT_ref[...], hb, NT_DIMS, preferred_element_type=F32)
    vT = lax.dot_general(wvT_ref[...], hb, NT_DIMS, preferred_element_type=F32)

    cosn, sina, sinb = cosn_ref[rows, :], sina_ref[rows, :], sinb_ref[rows, :]
    k_chunks = []
    for c in range(k.shape[1] // LANES):
        kc = k[:, c * LANES:(c + 1) * LANES]
        kr = (kc * cosn + pltpu.roll(kc, LANES - HALF, axis=1) * sina
              + pltpu.roll(kc, HALF, axis=1) * sinb)
        k_chunks.append(kr)
        k_ref[0, rows, c * LANES:(c + 1) * LANES] = kr.astype(BF16)

    for t in range(blocks_per_group):
        blk = slice(t * MOBA_BLOCK, (t + 1) * MOBA_BLOCK)
        mean = jnp.concatenate([jnp.sum(kr[blk], axis=0, keepdims=True) for kr in k_chunks],
                               axis=1) * (1.0 / MOBA_BLOCK)
        km_ref[pl.ds(blk0 + t, 1), :] = mean

    cos_t, sin_t = cost_ref[:, rows], sint_ref[:, rows]
    q_parts = []
    for hd in range(n_heads):
        x1 = qT[hd * HEAD_DIM:hd * HEAD_DIM + HALF]
        x2 = qT[hd * HEAD_DIM + HALF:(hd + 1) * HEAD_DIM]
        q_parts.append(x1 * cos_t - x2 * sin_t)
        q_parts.append(x2 * cos_t + x1 * sin_t)
    qs = jnp.concatenate(q_parts, axis=0) * Q_SCALE
    qs_hi = qs.astype(BF16)
    qT_ref[0, :, rows] = qs_hi
    ones_rows = jnp.where(lax.broadcasted_iota(jnp.int32, (V_ROWS - HEAD_DIM, tg), 0) == 0, 1.0, 0.0)
    v_parts = []
    for hd in range(n_heads):
        v_parts += [vT[hd * HEAD_DIM:(hd + 1) * HEAD_DIM], ones_rows]
    v_aug = jnp.concatenate(v_parts, axis=0).astype(BF16)
    for t in range(blocks_per_group):
        vT_ref[0, tile_blk0 + t] = v_aug[:, t * MOBA_BLOCK:(t + 1) * MOBA_BLOCK]

    km = km_ref[...]
    km_rep = jnp.concatenate([km] * n_heads, axis=0)
    r_head = lax.broadcasted_iota(jnp.int32, km_rep.shape, 0) // n_blocks
    l_head = lax.broadcasted_iota(jnp.int32, km_rep.shape, 1) // HEAD_DIM
    km_bd = jnp.where(r_head == l_head, km_rep, 0.0)
    km_hi = km_bd.astype(BF16)
    km_lo = (km_bd - km_hi.astype(F32)).astype(BF16)
    qs_lo = (qs - qs_hi.astype(F32)).astype(BF16)
    gate = (jnp.dot(km_hi, qs_hi, preferred_element_type=F32)
            + jnp.dot(km_hi, qs_lo, preferred_element_type=F32)
            + jnp.dot(km_lo, qs_hi, preferred_element_type=F32))

    q_blk = (pos0 + lax.broadcasted_iota(jnp.int32, (n_blocks, tg), 1)) // MOBA_BLOCK
    row_j = lax.broadcasted_iota(jnp.int32, (n_blocks, tg), 0)
    for hd in range(n_heads):
        g = gate[hd * n_blocks:(hd + 1) * n_blocks]
        rank = jnp.zeros((n_blocks, tg), jnp.int32)
        for j in range(n_blocks):
            gj = jnp.broadcast_to(g[j:j + 1, :], g.shape)
            ahead = (gj > g) | ((gj == g) & (j < row_j))
            rank = rank + jnp.where(ahead & (j < q_blk), 1, 0)
        chosen = (row_j < q_blk) & (rank < MOBA_TOPK)
        sel_ref[0, hd * n_blocks:(hd + 1) * n_blocks, rows] = jnp.where(chosen, 0.0, -jnp.inf)

    ext = jnp.concatenate([tail_ref[...], u], axis=0)
    tail_ref[...] = u[tg - MAX_WINDOW:, :]
    pos1 = pos0 + lax.broadcasted_iota(jnp.int32, (tg, 1), 0) + 1
    for gi, w in enumerate(POOL_WINDOWS):
        cols = slice(gi * POOL_GROUP, (gi + 1) * POOL_GROUP)
        win = ext[:, cols]
        span = 1
        while span < w:
            win = win + pltpu.roll(win, span, axis=0)
            span *= 2
        count = jnp.minimum(pos1, w).astype(F32)
        d = win[MAX_WINDOW:, :] / count - u[:, cols]
        y = jnp.dot(d.astype(BF16), wpool_ref[gi], preferred_element_type=F32)
        pool_ref[0, rows, cols] = (y * pscale_ref[:, cols]).astype(BF16)


def _mix_in(x, mod3, g_pre, wqT, wk, wvT, wu, wpool, pscale, tables, *, tm, sub_rows):
    B, S, D = x.shape
    A = wk.shape[1]
    P = wu.shape[1]
    n_heads = A // HEAD_DIM
    n_blocks = S // MOBA_BLOCK
    bpt = tm // MOBA_BLOCK
    cosn, sina, sinb, cost, sint = tables
    const = lambda *shape: pl.BlockSpec(shape, lambda b, i: (0,) * len(shape),
                                        pipeline_mode=pl.Buffered(1))
    kern = functools.partial(_mix_in_kernel, tm=tm, sub_rows=sub_rows, n_heads=n_heads,
                             n_blocks=n_blocks)
    return pl.pallas_call(
        kern,
        grid=(B, S // tm),
        in_specs=[
            pl.BlockSpec((1, tm, D), lambda b, i: (b, i, 0)),
            pl.BlockSpec((1, 6, D), lambda b, i: (b, 0, 0)),
            const(1, D),
            const(A, D), const(D, A), const(A, D), const(D, P),
            const(len(POOL_WINDOWS), POOL_GROUP, POOL_GROUP),
            const(1, P),
            pl.BlockSpec((tm, LANES), lambda b, i: (i, 0)),
            pl.BlockSpec((tm, LANES), lambda b, i: (i, 0)),
            pl.BlockSpec((tm, LANES), lambda b, i: (i, 0)),
            pl.BlockSpec((HALF, tm), lambda b, i: (0, i)),
            pl.BlockSpec((HALF, tm), lambda b, i: (0, i)),
        ],
        out_specs=[
            pl.BlockSpec((1, A, tm), lambda b, i: (b, 0, i)),
            pl.BlockSpec((1, tm, A), lambda b, i: (b, i, 0)),
            pl.BlockSpec((1, bpt, n_heads * V_ROWS, MOBA_BLOCK), lambda b, i: (b, i, 0, 0)),
            pl.BlockSpec((1, tm, P), lambda b, i: (b, i, 0)),
            pl.BlockSpec((1, n_heads * n_blocks, tm), lambda b, i: (b, 0, i)),
        ],
        out_shape=[
            jax.ShapeDtypeStruct((B, A, S), BF16),
            jax.ShapeDtypeStruct((B, S, A), BF16),
            jax.ShapeDtypeStruct((B, n_blocks, n_heads * V_ROWS, MOBA_BLOCK), BF16),
            jax.ShapeDtypeStruct((B, S, P), BF16),
            jax.ShapeDtypeStruct((B, n_heads * n_blocks, S), F32),
        ],
        scratch_shapes=[pltpu.VMEM((n_blocks, A), F32), pltpu.VMEM((MAX_WINDOW, P), F32)],
        compiler_params=pltpu.CompilerParams(dimension_semantics=("parallel", "arbitrary"),
                                             vmem_limit_bytes=VMEM_LIMIT),
        name="mix_in",
    )(x, mod3, g_pre, wqT, wk, wvT, wu, wpool, pscale, cosn, sina, sinb, cost, sint)


def _attn_kernel(qT_ref, k_ref, vT_ref, sel_ref, o_ref, qz_ref, m_ref, acc_ref,
                 s0_ref, alpha0_ref, shift0_ref, s1_ref, alpha1_ref, shift1_ref,
                 *, rows, n_heads, n_blocks):
    i = pl.program_id(1)
    tq = MOBA_BLOCK
    zeros = jnp.zeros((HEAD_DIM, tq), BF16)
    items = [(r, hd) for r in range(rows) for hd in range(n_heads)]
    for it, (r, hd) in enumerate(items):
        qh = qT_ref[r, hd * HEAD_DIM:(hd + 1) * HEAD_DIM, :]
        qz_ref[it] = jnp.concatenate([qh, zeros] if hd % PAIR == 0 else [zeros, qh], axis=0)
    acc_ref[...] = jnp.zeros_like(acc_ref)
    buffers = ((s0_ref, alpha0_ref, shift0_ref), (s1_ref, alpha1_ref, shift1_ref))

    def pass1(j, own, dst):
        s_ref, alpha_ref, shift_ref = dst
        keys = pl.ds(pl.multiple_of(j * tq, tq), tq)
        for it, (r, hd) in enumerate(items):
            c = hd // PAIR
            s = jnp.dot(k_ref[r, keys, c * LANES:(c + 1) * LANES], qz_ref[it],
                        preferred_element_type=F32)
            if own:
                key_i = lax.broadcasted_iota(jnp.int32, (tq, tq), 0)
                qry_i = lax.broadcasted_iota(jnp.int32, (tq, tq), 1)
                s = jnp.where(key_i <= qry_i, s, -jnp.inf)
            s_ref[it] = s
            cmax = jnp.max(s, axis=0, keepdims=True)
            if own:
                m_ref[it] = cmax
                alpha_ref[it] = jnp.zeros_like(cmax)
                shift_ref[it] = cmax
            else:
                bias = sel_ref[r, pl.ds(hd * n_blocks + j, 1), :]
                m_old = m_ref[it]
                m_new = jnp.maximum(m_old, cmax + bias)
                m_ref[it] = m_new
                alpha_ref[it] = jnp.exp2(m_old - m_new)
                shift_ref[it] = m_new - bias

    def pass2(j, src):
        s_ref, alpha_ref, shift_ref = src
        for it, (r, hd) in enumerate(items):
            p = jnp.exp2((s_ref[it] - shift_ref[it]).astype(BF16))
            pv = jnp.dot(vT_ref[r, j, hd * V_ROWS:(hd + 1) * V_ROWS, :], p,
                         preferred_element_type=F32)
            acc_ref[it] = alpha_ref[it] * acc_ref[it] + pv

    def nth_block(n):
        return jnp.where(n == 0, i, n - 1)

    pass1(i, True, buffers[0])

    def two_turns(u, carry):
        t = 2 * u
        pass2(nth_block(t), buffers[0])
        pass1(t, False, buffers[1])
        pass1(t + 1, False, buffers[0])
        pass2(t, buffers[1])
        return carry

    lax.fori_loop(0, i // 2, two_turns, 0)

    @pl.when(i % 2 == 1)
    def _():
        pass1(i - 1, False, buffers[1])
        pass2(nth_block(i - 1), buffers[0])
        pass2(i - 1, buffers[1])

    @pl.when(i % 2 == 0)
    def _():
        pass2(nth_block(i), buffers[0])

    for r in range(rows):
        for c in range(n_heads // PAIR):
            pair = [acc_ref[it, :HEAD_DIM] * (1.0 / acc_ref[it, HEAD_DIM:HEAD_DIM + 1])
                    for it in range(r * n_heads + c * PAIR, r * n_heads + (c + 1) * PAIR)]
            o_ref[r, :, c * LANES:(c + 1) * LANES] = jnp.concatenate(pair, axis=0).T.astype(BF16)


def _moba_attn(qT, k, vT, sel, *, rows):
    B, A, S = qT.shape
    n_heads = A // HEAD_DIM
    n_blocks = S // MOBA_BLOCK
    n_items = rows * n_heads
    kern = functools.partial(_attn_kernel, rows=rows, n_heads=n_heads, n_blocks=n_blocks)
    stage = [pltpu.VMEM((n_items, MOBA_BLOCK, MOBA_BLOCK), F32),
             pltpu.VMEM((n_items, 1, MOBA_BLOCK), F32),
             pltpu.VMEM((n_items, 1, MOBA_BLOCK), F32)]
    return pl.pallas_call(
        kern,
        grid=(B // rows, n_blocks),
        in_specs=[
            pl.BlockSpec((rows, A, MOBA_BLOCK), lambda b, i: (b, 0, i)),
            pl.BlockSpec((rows, S, A), lambda b, i: (b, 0, 0)),
            pl.BlockSpec((rows, n_blocks, n_heads * V_ROWS, MOBA_BLOCK), lambda b, i: (b, 0, 0, 0)),
            pl.BlockSpec((rows, n_heads * n_blocks, MOBA_BLOCK), lambda b, i: (b, 0, i)),
        ],
        out_specs=pl.BlockSpec((rows, MOBA_BLOCK, A), lambda b, i: (b, i, 0)),
        out_shape=jax.ShapeDtypeStruct((B, S, A), BF16),
        scratch_shapes=[
            pltpu.VMEM((n_items, LANES, MOBA_BLOCK), BF16),
            pltpu.VMEM((n_items, 1, MOBA_BLOCK), F32),
            pltpu.VMEM((n_items, V_ROWS, MOBA_BLOCK), F32),
        ] + stage + stage,
        compiler_params=pltpu.CompilerParams(dimension_semantics=("parallel", "arbitrary"),
                                             vmem_limit_bytes=VMEM_LIMIT),
        name="moba_attn",
    )(qT, k, vT, sel)


def _out_mlp_kernel(x_ref, attn_ref, pool_ref, mod_ref, g_post_ref, g_pre2_ref, g_post2_ref,
                    wo_ref, wup_ref, wdn_ref, o_ref, *, sub_rows, ff_chunk):
    A = attn_ref.shape[2]
    groups = [slice(r * sub_rows, (r + 1) * sub_rows) for r in range(x_ref.shape[1] // sub_rows)]
    ys = [jnp.dot(attn_ref[0, rows, :], wo_ref[:A, :], preferred_element_type=F32)
          + jnp.dot(pool_ref[0, rows, :], wo_ref[A:, :], preferred_element_type=F32)
          for rows in groups]
    for rows, y in zip(groups, ys):
        x = x_ref[0, rows, :]
        x1 = x + mod_ref[0, 2:3, :] * (_rms(y) * g_post_ref[...])
        h = (_rms(x1) * g_pre2_ref[...]) * (1.0 + mod_ref[0, 4:5, :]) + mod_ref[0, 3:4, :]
        hb = h.astype(BF16)
        y2 = jnp.zeros_like(x)
        for c in range(wup_ref.shape[1] // ff_chunk):
            cols = slice(c * ff_chunk, (c + 1) * ff_chunk)
            up = jnp.dot(hb, wup_ref[:, cols], preferred_element_type=F32)
            act = jnp.square(jnp.maximum(up, 0.0)).astype(BF16)
            y2 = y2 + jnp.dot(act, wdn_ref[cols, :], preferred_element_type=F32)
        o_ref[0, rows, :] = x1 + mod_ref[0, 5:6, :] * (_rms(y2) * g_post2_ref[...])


def _out_mlp(x, attn, pool, mod3, g_post, g_pre2, g_post2, wo, wup, wdn, *, tm, sub_rows,
             ff_chunk):
    B, S, D = x.shape
    A, P, FF = attn.shape[2], pool.shape[2], wup.shape[1]
    const = lambda *shape: pl.BlockSpec(shape, lambda b, i: (0,) * len(shape),
                                        pipeline_mode=pl.Buffered(1))
    return pl.pallas_call(
        functools.partial(_out_mlp_kernel, sub_rows=sub_rows, ff_chunk=ff_chunk),
        grid=(B, S // tm),
        in_specs=[
            pl.BlockSpec((1, tm, D), lambda b, i: (b, i, 0)),
            pl.BlockSpec((1, tm, A), lambda b, i: (b, i, 0)),
            pl.BlockSpec((1, tm, P), lambda b, i: (b, i, 0)),
            pl.BlockSpec((1, 6, D), lambda b, i: (b, 0, 0)),
            const(1, D), const(1, D), const(1, D),
            const(A + P, D), const(D, FF), const(FF, D),
        ],
        out_specs=pl.BlockSpec((1, tm, D), lambda b, i: (b, i, 0)),
        out_shape=jax.ShapeDtypeStruct((B, S, D), F32),
        compiler_params=pltpu.CompilerParams(dimension_semantics=("parallel", "parallel"),
                                             vmem_limit_bytes=VMEM_LIMIT),
        name="out_mlp",
    )(x, attn, pool, mod3, g_post, g_pre2, g_post2, wo, wup, wdn)


def _rope_tables(S):
    inv_freq = 1.0 / (ROPE_THETA ** (jnp.arange(HALF, dtype=F32) * (2.0 / HEAD_DIM)))
    ang = jnp.arange(S, dtype=jnp.int32).astype(F32)[:, None] * inv_freq[None, :]
    cos, sin = jnp.cos(ang), jnp.sin(ang)
    zero = jnp.zeros_like(sin)
    reps = LANES // HEAD_DIM
    cosn = jnp.tile(jnp.concatenate([cos, cos], axis=1), (1, reps))
    sina = jnp.tile(jnp.concatenate([-sin, zero], axis=1), (1, reps))
    sinb = jnp.tile(jnp.concatenate([zero, sin], axis=1), (1, reps))
    return cosn, sina, sinb, cos.T, sin.T


def kernel(x, c, w_ada, b_ada, g_mix_pre, g_mix_post, w_in, w_pool, pool_scale, w_out,
           g_mlp_pre, g_mlp_post, w_up, w_down):
    B, S, D = x.shape
    depth = w_ada.shape[0]
    P = pool_scale.shape[1]
    A = w_out.shape[1] - P
    tables = _rope_tables(S)
    for l in range(depth):
        mod3 = _adaln_mod(c, w_ada[l], b_ada[l]).reshape(B, 6, D)
        w_in_b = w_in[l].astype(BF16)
        wqT = w_in_b[:, :A].T
        wk = w_in_b[:, A:2 * A]
        wvT = w_in_b[:, 2 * A:3 * A].T
        wu = w_in_b[:, 3 * A:]
        qT, k, vT, pool, sel = _mix_in(
            x, mod3, g_mix_pre[l][None, :], wqT, wk, wvT, wu, w_pool[l].astype(BF16),
            pool_scale[l][None, :], tables, tm=1024, sub_rows=512)
        attn = _moba_attn(qT, k, vT, sel, rows=2 if B % 2 == 0 else 1)
        x = _out_mlp(x, attn, pool, mod3, g_mix_post[l][None, :], g_mlp_pre[l][None, :],
                     g_mlp_post[l][None, :], w_out[l].astype(BF16), w_up[l].astype(BF16),
                     w_down[l].astype(BF16), tm=1024, sub_rows=256, ff_chunk=1024)
    return x
```

```python
import functools

import jax
import jax.numpy as jnp
from jax import lax
from jax.experimental import pallas as pl
from jax.experimental.pallas import tpu as pltpu

F32 = jnp.float32
BF16 = jnp.bfloat16

HEAD_DIM = 64
HALF = HEAD_DIM // 2
MOBA_BLOCK = 256
MOBA_TOPK = 3
POOL_WINDOWS = (2, 4, 8, 16)
POOL_GROUP = 128
MAX_WINDOW = max(POOL_WINDOWS)
ROPE_THETA = 10000.0
NORM_EPS = 1e-6
Q_SCALE = HEAD_DIM ** -0.5 * 1.4426950408889634
BF16_ROWS = 16
V_ROWS = HEAD_DIM + BF16_ROWS
LANES = 128
PAIR = LANES // HEAD_DIM
VMEM_LIMIT = 56 * 1024 * 1024

NT_DIMS = (((1,), (1,)), ((), ()))


def _rms(x):
    return x * lax.rsqrt(jnp.mean(x * x, axis=-1, keepdims=True) + NORM_EPS)


def _mod_kernel(c_ref, w_ref, b_ref, o_ref):
    c = c_ref[...]
    act = c * jax.nn.sigmoid(c)
    o_ref[...] = jnp.dot(act, w_ref[...], precision=lax.Precision.HIGHEST,
                         preferred_element_type=F32) + b_ref[...]


def _adaln_mod(c, w_ada, b_ada):
    B, D = c.shape
    n_out = w_ada.shape[1]
    return pl.pallas_call(
        _mod_kernel,
        grid=(n_out // D,),
        in_specs=[pl.BlockSpec((B, D), lambda n: (0, 0)),
                  pl.BlockSpec((D, D), lambda n: (0, n)),
                  pl.BlockSpec((1, D), lambda n: (0, n))],
        out_specs=pl.BlockSpec((B, D), lambda n: (0, n)),
        out_shape=jax.ShapeDtypeStruct((B, n_out), F32),
        name="adaln_mod",
    )(c, w_ada, b_ada.reshape(1, n_out))


def _mix_in_kernel(x_ref, mod_ref, g_ref, wqT_ref, wk_ref, wvT_ref, wu_ref, wpool_ref, pscale_ref,
                   cosn_ref, sina_ref, sinb_ref, cost_ref, sint_ref,
                   qT_ref, k_ref, vT_ref, pool_ref, sel_ref,
                   km_ref, tail_ref, *, tm, sub_rows, n_heads, n_blocks):
    i = pl.program_id(1)
    blocks_per_group = sub_rows // MOBA_BLOCK

    @pl.when(i == 0)
    def _():
        km_ref[...] = jnp.zeros_like(km_ref)
        tail_ref[...] = jnp.zeros_like(tail_ref)

    groups = [slice(r * sub_rows, (r + 1) * sub_rows) for r in range(tm // sub_rows)]
    hbs = []
    for rows in groups:
        x = x_ref[0, rows, :]
        h = (_rms(x) * g_ref[...]) * (1.0 + mod_ref[0, 1:2, :]) + mod_ref[0, 0:1, :]
        hbs.append(h.astype(BF16))
    for gr, (rows, hb) in enumerate(zip(groups, hbs)):
        _mix_in_group(i * tm + gr * sub_rows, i * (tm // MOBA_BLOCK) + gr * blocks_per_group,
                      rows, gr * blocks_per_group, hb,
                      wqT_ref, wk_ref, wvT_ref, wu_ref, wpool_ref, pscale_ref,
                      cosn_ref, sina_ref, sinb_ref, cost_ref, sint_ref,
                      qT_ref, k_ref, vT_ref, pool_ref, sel_ref, km_ref, tail_ref,
                      n_heads=n_heads, n_blocks=n_blocks)


def _mix_in_group(pos0, blk0, rows, tile_blk0, hb,
                  wqT_ref, wk_ref, wvT_ref, wu_ref, wpool_ref, pscale_ref,
                  cosn_ref, sina_ref, sinb_ref, cost_ref, sint_ref,
                  qT_ref, k_ref, vT_ref, pool_ref, sel_ref, km_ref, tail_ref, *, n_heads, n_blocks):
    tg = hb.shape[0]
    blocks_per_group = tg // MOBA_BLOCK
    k = jnp.dot(hb, wk_ref[...], preferred_element_type=F32)
    u = jnp.dot(hb, wu_ref[...], preferred_element_type=F32)
    qT = lax.dot_general(wqT_ref[...], hb, NT_DIMS, preferred_element_type=F32)
    vT = lax.dot_general(wvT_ref[...], hb, NT_DIMS, preferred_element_type=F32)

    cosn, sina, sinb = cosn_ref[rows, :], sina_ref[rows, :], sinb_ref[rows, :]
    k_chunks = []
    for c in range(k.shape[1] // LANES):
        kc = k[:, c * LANES:(c + 1) * LANES]
        kr = (kc * cosn + pltpu.roll(kc, LANES - HALF, axis=1) * sina
              + pltpu.roll(kc, HALF, axis=1) * sinb)
        k_chunks.append(kr)
        k_ref[0, rows, c * LANES:(c + 1) * LANES] = kr.astype(BF16)

    for t in range(blocks_per_group):
        blk = slice(t * MOBA_BLOCK, (t + 1) * MOBA_BLOCK)
        mean = jnp.concatenate([jnp.sum(kr[blk], axis=0, keepdims=True) for kr in k_chunks],
                               axis=1) * (1.0 / MOBA_BLOCK)
        km_ref[pl.ds(blk0 + t, 1), :] = mean

    cos_t, sin_t = cost_ref[:, rows], sint_ref[:, rows]
    q_parts = []
    for hd in range(n_heads):
        x1 = qT[hd * HEAD_DIM:hd * HEAD_DIM + HALF]
        x2 = qT[hd * HEAD_DIM + HALF:(hd + 1) * HEAD_DIM]
        q_parts.append(x1 * cos_t - x2 * sin_t)
        q_parts.append(x2 * cos_t + x1 * sin_t)
    qs = jnp.concatenate(q_parts, axis=0) * Q_SCALE
    qs_hi = qs.astype(BF16)
    qT_ref[0, :, rows] = qs_hi
    ones_rows = jnp.where(lax.broadcasted_iota(jnp.int32, (V_ROWS - HEAD_DIM, tg), 0) == 0, 1.0, 0.0)
    v_parts = []
    for hd in range(n_heads):
        v_parts += [vT[hd * HEAD_DIM:(hd + 1) * HEAD_DIM], ones_rows]
    v_aug = jnp.concatenate(v_parts, axis=0).astype(BF16)
    for t in range(blocks_per_group):
        vT_ref[0, tile_blk0 + t] = v_aug[:, t * MOBA_BLOCK:(t + 1) * MOBA_BLOCK]

    km = km_ref[...]
    km_rep = jnp.concatenate([km] * n_heads, axis=0)
    r_head = lax.broadcasted_iota(jnp.int32, km_rep.shape, 0) // n_blocks
    l_head = lax.broadcasted_iota(jnp.int32, km_rep.shape, 1) // HEAD_DIM
    km_bd = jnp.where(r_head == l_head, km_rep, 0.0)
    km_hi = km_bd.astype(BF16)
    km_lo = (km_bd - km_hi.astype(F32)).astype(BF16)
    qs_lo = (qs - qs_hi.astype(F32)).astype(BF16)
    gate = (jnp.dot(km_hi, qs_hi, preferred_element_type=F32)
            + jnp.dot(km_hi, qs_lo, preferred_element_type=F32)
            + jnp.dot(km_lo, qs_hi, preferred_element_type=F32))

    q_blk = (pos0 + lax.broadcasted_iota(jnp.int32, (n_blocks, tg), 1)) // MOBA_BLOCK
    row_j = lax.broadcasted_iota(jnp.int32, (n_blocks, tg), 0)
    for hd in range(n_heads):
        g = gate[hd * n_blocks:(hd + 1) * n_blocks]
        rank = jnp.zeros((n_blocks, tg), jnp.int32)
        for j in range(n_blocks):
            gj = jnp.broadcast_to(g[j:j + 1, :], g.shape)
            ahead = (gj > g) | ((gj == g) & (j < row_j))
            rank = rank + jnp.where(ahead & (j < q_blk), 1, 0)
        chosen = (row_j < q_blk) & (rank < MOBA_TOPK)
        sel_ref[0, hd * n_blocks:(hd + 1) * n_blocks, rows] = jnp.where(chosen, 0.0, -jnp.inf)

    ext = jnp.concatenate([tail_ref[...], u], axis=0)
    tail_ref[...] = u[tg - MAX_WINDOW:, :]
    pos1 = pos0 + lax.broadcasted_iota(jnp.int32, (tg, 1), 0) + 1
    for gi, w in enumerate(POOL_WINDOWS):
        cols = slice(gi * POOL_GROUP, (gi + 1) * POOL_GROUP)
        win = ext[:, cols]
        span = 1
        while span < w:
            win = win + pltpu.roll(win, span, axis=0)
            span *= 2
        count = jnp.minimum(pos1, w).astype(F32)
        d = win[MAX_WINDOW:, :] / count - u[:, cols]
        y = jnp.dot(d.astype(BF16), wpool_ref[gi], preferred_element_type=F32)
        pool_ref[0, rows, cols] = (y * pscale_ref[:, cols]).astype(BF16)


def _mix_in(x, mod3, g_pre, wqT, wk, wvT, wu, wpool, pscale, tables, *, tm, sub_rows):
    B, S, D = x.shape
    A = wk.shape[1]
    P = wu.shape[1]
    n_heads = A // HEAD_DIM
    n_blocks = S // MOBA_BLOCK
    bpt = tm // MOBA_BLOCK
    cosn, sina, sinb, cost, sint = tables
    const = lambda *shape: pl.BlockSpec(shape, lambda b, i: (0,) * len(shape),
                                        pipeline_mode=pl.Buffered(1))
    kern = functools.partial(_mix_in_kernel, tm=tm, sub_rows=sub_rows, n_heads=n_heads,
                             n_blocks=n_blocks)
    return pl.pallas_call(
        kern,
        grid=(B, S // tm),
        in_specs=[
            pl.BlockSpec((1, tm, D), lambda b, i: (b, i, 0)),
            pl.BlockSpec((1, 6, D), lambda b, i: (b, 0, 0)),
            const(1, D),
            const(A, D), const(D, A), const(A, D), const(D, P),
            const(len(POOL_WINDOWS), POOL_GROUP, POOL_GROUP),
            const(1, P),
            pl.BlockSpec((tm, LANES), lambda b, i: (i, 0)),
            pl.BlockSpec((tm, LANES), lambda b, i: (i, 0)),
            pl.BlockSpec((tm, LANES), lambda b, i: (i, 0)),
            pl.BlockSpec((HALF, tm), lambda b, i: (0, i)),
            pl.BlockSpec((HALF, tm), lambda b, i: (0, i)),
        ],
        out_specs=[
            pl.BlockSpec((1, A, tm), lambda b, i: (b, 0, i)),
            pl.BlockSpec((1, tm, A), lambda b, i: (b, i, 0)),
            pl.BlockSpec((1, bpt, n_heads * V_ROWS, MOBA_BLOCK), lambda b, i: (b, i, 0, 0)),
            pl.BlockSpec((1, tm, P), lambda b, i: (b, i, 0)),
            pl.BlockSpec((1, n_heads * n_blocks, tm), lambda b, i: (b, 0, i)),
        ],
        out_shape=[
            jax.ShapeDtypeStruct((B, A, S), BF16),
            jax.ShapeDtypeStruct((B, S, A), BF16),
            jax.ShapeDtypeStruct((B, n_blocks, n_heads * V_ROWS, MOBA_BLOCK), BF16),
            jax.ShapeDtypeStruct((B, S, P), BF16),
            jax.ShapeDtypeStruct((B, n_heads * n_blocks, S), F32),
        ],
        scratch_shapes=[pltpu.VMEM((n_blocks, A), F32), pltpu.VMEM((MAX_WINDOW, P), F32)],
        compiler_params=pltpu.CompilerParams(dimension_semantics=("parallel", "arbitrary"),
                                             vmem_limit_bytes=VMEM_LIMIT),
        name="mix_in",
    )(x, mod3, g_pre, wqT, wk, wvT, wu, wpool, pscale, cosn, sina, sinb, cost, sint)


def _attn_kernel(qa_ref, qb_ref, k_ref, vT_ref, sela_ref, selb_ref, o_ref,
                 qz_ref, sel_ref, m_ref, acc_ref,
                 s0_ref, alpha0_ref, shift0_ref, s1_ref, alpha1_ref, shift1_ref,
                 *, rows, n_heads, n_blocks):
    p = pl.program_id(1)
    q_blocks = (p, n_blocks - 1 - p)
    n_past = n_blocks - 1
    tq = MOBA_BLOCK
    items = [(r, hd) for r in range(rows) for hd in range(n_heads)]
    n_items = len(items)
    zeros = jnp.zeros((HEAD_DIM, tq), BF16)
    for w, q_ref in enumerate((qa_ref, qb_ref)):
        for it, (r, hd) in enumerate(items):
            qh = q_ref[r, hd * HEAD_DIM:(hd + 1) * HEAD_DIM, :]
            qz_ref[w * n_items + it] = jnp.concatenate(
                [qh, zeros] if hd % PAIR == 0 else [zeros, qh], axis=0)
    sel_ref[0] = sela_ref[...]
    sel_ref[1] = selb_ref[...]
    acc_ref[...] = jnp.zeros_like(acc_ref)
    buffers = ((s0_ref, alpha0_ref, shift0_ref), (s1_ref, alpha1_ref, shift1_ref))

    def pass1(w, j, own, dst):
        s_ref, alpha_ref, shift_ref = dst
        keys = pl.ds(pl.multiple_of(j * tq, tq), tq)
        qz_w, m_w = qz_ref.at[pl.ds(w * n_items, n_items)], m_ref.at[pl.ds(w * n_items, n_items)]
        for it, (r, hd) in enumerate(items):
            c = hd // PAIR
            s = jnp.dot(k_ref[r, keys, c * LANES:(c + 1) * LANES], qz_w[it],
                        preferred_element_type=F32)
            if own:
                key_i = lax.broadcasted_iota(jnp.int32, (tq, tq), 0)
                qry_i = lax.broadcasted_iota(jnp.int32, (tq, tq), 1)
                s = jnp.where(key_i <= qry_i, s, -jnp.inf)
            s_ref[it] = s
            cmax = jnp.max(s, axis=0, keepdims=True)
            if own:
                m_w[it] = cmax
                alpha_ref[it] = jnp.zeros_like(cmax)
                shift_ref[it] = cmax
            else:
                bias = sel_ref[w, r, pl.ds(hd * n_blocks + j, 1), :]
                m_old = m_w[it]
                m_new = jnp.maximum(m_old, cmax + bias)
                m_w[it] = m_new
                alpha_ref[it] = jnp.exp2(m_old - m_new)
                shift_ref[it] = m_new - bias

    def pass2(w, j, src):
        s_ref, alpha_ref, shift_ref = src
        acc_w = acc_ref.at[pl.ds(w * n_items, n_items)]
        for it, (r, hd) in enumerate(items):
            prob = jnp.exp2((s_ref[it] - shift_ref[it]).astype(BF16))
            pv = jnp.dot(vT_ref[r, j, hd * V_ROWS:(hd + 1) * V_ROWS, :], prob,
                         preferred_element_type=F32)
            acc_w[it] = alpha_ref[it] * acc_w[it] + pv

    def past(e):
        first = e < q_blocks[0]
        return jnp.where(first, 0, 1), jnp.where(first, e, e - q_blocks[0])

    pass1(0, q_blocks[0], True, buffers[0])
    pass2(0, q_blocks[0], buffers[0])
    pass1(1, q_blocks[1], True, buffers[1])
    pass2(1, q_blocks[1], buffers[1])
    pass1(*past(0), False, buffers[0])

    def two_turns(u, carry):
        e = 2 * u
        pass2(*past(e), buffers[0])
        pass1(*past(e + 1), False, buffers[1])
        pass2(*past(e + 1), buffers[1])
        pass1(*past(e + 2), False, buffers[0])
        return carry

    lax.fori_loop(0, (n_past - 1) // 2, two_turns, 0)
    pass2(*past(n_past - 1), buffers[0])

    for w in range(2):
        for r in range(rows):
            for c in range(n_heads // PAIR):
                pair = [acc_ref[it, :HEAD_DIM] * (1.0 / acc_ref[it, HEAD_DIM:HEAD_DIM + 1])
                        for it in range(w * n_items + r * n_heads + c * PAIR,
                                        w * n_items + r * n_heads + (c + 1) * PAIR)]
                o_ref[r, w, 0, :, c * LANES:(c + 1) * LANES] = (
                    jnp.concatenate(pair, axis=0).T.astype(BF16))


def _moba_attn(qT, k, vT, sel, *, rows):
    B, A, S = qT.shape
    n_heads = A // HEAD_DIM
    n_blocks = S // MOBA_BLOCK
    assert n_blocks % 2 == 0 and (n_blocks - 1) % 2 == 1
    n_items = rows * n_heads
    kern = functools.partial(_attn_kernel, rows=rows, n_heads=n_heads, n_blocks=n_blocks)
    stage = [pltpu.VMEM((n_items, MOBA_BLOCK, MOBA_BLOCK), F32),
             pltpu.VMEM((n_items, 1, MOBA_BLOCK), F32),
             pltpu.VMEM((n_items, 1, MOBA_BLOCK), F32)]
    last = n_blocks - 1
    return pl.pallas_call(
        kern,
        grid=(B // rows, n_blocks // 2),
        in_specs=[
            pl.BlockSpec((rows, A, MOBA_BLOCK), lambda b, p: (b, 0, p)),
            pl.BlockSpec((rows, A, MOBA_BLOCK), lambda b, p: (b, 0, last - p)),
            pl.BlockSpec((rows, S, A), lambda b, p: (b, 0, 0)),
            pl.BlockSpec((rows, n_blocks, n_heads * V_ROWS, MOBA_BLOCK), lambda b, p: (b, 0, 0, 0)),
            pl.BlockSpec((rows, n_heads * n_blocks, MOBA_BLOCK), lambda b, p: (b, 0, p)),
            pl.BlockSpec((rows, n_heads * n_blocks, MOBA_BLOCK), lambda b, p: (b, 0, last - p)),
        ],
        out_specs=pl.BlockSpec((rows, 2, 1, MOBA_BLOCK, A), lambda b, p: (b, 0, p, 0, 0)),
        out_shape=jax.ShapeDtypeStruct((B, 2, n_blocks // 2, MOBA_BLOCK, A), BF16),
        scratch_shapes=[
            pltpu.VMEM((2 * n_items, LANES, MOBA_BLOCK), BF16),
            pltpu.VMEM((2, rows, n_heads * n_blocks, MOBA_BLOCK), F32),
            pltpu.VMEM((2 * n_items, 1, MOBA_BLOCK), F32),
            pltpu.VMEM((2 * n_items, V_ROWS, MOBA_BLOCK), F32),
        ] + stage + stage,
        compiler_params=pltpu.CompilerParams(dimension_semantics=("parallel", "arbitrary"),
                                             vmem_limit_bytes=VMEM_LIMIT),
        name="moba_attn",
    )(qT, qT, k, vT, sel, sel)


def _out_mlp_kernel(x_ref, attn_ref, pool_ref, mod_ref, g_post_ref, g_pre2_ref, g_post2_ref,
                    wo_ref, wup_ref, wdn_ref, o_ref, *, sub_rows, ff_chunk):
    A = attn_ref.shape[4]
    n_groups = x_ref.shape[1] // sub_rows
    half = pl.program_id(1)
    groups = [slice(r * sub_rows, (r + 1) * sub_rows) for r in range(n_groups)]
    ys = [jnp.dot(attn_ref[0, 0, jnp.where(half == 0, r, n_groups - 1 - r)], wo_ref[:A, :],
                  preferred_element_type=F32)
          + jnp.dot(pool_ref[0, rows, :], wo_ref[A:, :], preferred_element_type=F32)
          for r, rows in enumerate(groups)]
    for rows, y in zip(groups, ys):
        x = x_ref[0, rows, :]
        x1 = x + mod_ref[0, 2:3, :] * (_rms(y) * g_post_ref[...])
        h = (_rms(x1) * g_pre2_ref[...]) * (1.0 + mod_ref[0, 4:5, :]) + mod_ref[0, 3:4, :]
        hb = h.astype(BF16)
        y2 = jnp.zeros_like(x)
        for c in range(wup_ref.shape[1] // ff_chunk):
            cols = slice(c * ff_chunk, (c + 1) * ff_chunk)
            up = jnp.dot(hb, wup_ref[:, cols], preferred_element_type=F32)
            act = jnp.square(jnp.maximum(up, 0.0)).astype(BF16)
            y2 = y2 + jnp.dot(act, wdn_ref[cols, :], preferred_element_type=F32)
        o_ref[0, rows, :] = x1 + mod_ref[0, 5:6, :] * (_rms(y2) * g_post2_ref[...])


def _out_mlp(x, attn, pool, mod3, g_post, g_pre2, g_post2, wo, wup, wdn, *, tm, sub_rows,
             ff_chunk):
    B, S, D = x.shape
    A, P, FF = attn.shape[4], pool.shape[2], wup.shape[1]
    assert attn.shape[1:4] == (S // tm, tm // sub_rows, sub_rows)
    const = lambda *shape: pl.BlockSpec(shape, lambda b, i: (0,) * len(shape),
                                        pipeline_mode=pl.Buffered(1))
    return pl.pallas_call(
        functools.partial(_out_mlp_kernel, sub_rows=sub_rows, ff_chunk=ff_chunk),
        grid=(B, S // tm),
        in_specs=[
            pl.BlockSpec((1, tm, D), lambda b, i: (b, i, 0)),
            pl.BlockSpec((1, 1) + attn.shape[2:], lambda b, i: (b, i, 0, 0, 0)),
            pl.BlockSpec((1, tm, P), lambda b, i: (b, i, 0)),
            pl.BlockSpec((1, 6, D), lambda b, i: (b, 0, 0)),
            const(1, D), const(1, D), const(1, D),
            const(A + P, D), const(D, FF), const(FF, D),
        ],
        out_specs=pl.BlockSpec((1, tm, D), lambda b, i: (b, i, 0)),
        out_shape=jax.ShapeDtypeStruct((B, S, D), F32),
        compiler_params=pltpu.CompilerParams(dimension_semantics=("parallel", "parallel"),
                                             vmem_limit_bytes=VMEM_LIMIT),
        name="out_mlp",
    )(x, attn, pool, mod3, g_post, g_pre2, g_post2, wo, wup, wdn)


def _rope_tables(S):
    inv_freq = 1.0 / (ROPE_THETA ** (jnp.arange(HALF, dtype=F32) * (2.0 / HEAD_DIM)))
    ang = jnp.arange(S, dtype=jnp.int32).astype(F32)[:, None] * inv_freq[None, :]
    cos, sin = jnp.cos(ang), jnp.sin(ang)
    zero = jnp.zeros_like(sin)
    reps = LANES // HEAD_DIM
    cosn = jnp.tile(jnp.concatenate([cos, cos], axis=1), (1, reps))
    sina = jnp.tile(jnp.concatenate([-sin, zero], axis=1), (1, reps))
    sinb = jnp.tile(jnp.concatenate([zero, sin], axis=1), (1, reps))
    return cosn, sina, sinb, cos.T, sin.T


def kernel(x, c, w_ada, b_ada, g_mix_pre, g_mix_post, w_in, w_pool, pool_scale, w_out,
           g_mlp_pre, g_mlp_post, w_up, w_down):
    B, S, D = x.shape
    depth = w_ada.shape[0]
    P = pool_scale.shape[1]
    A = w_out.shape[1] - P
    tables = _rope_tables(S)
    for l in range(depth):
        mod3 = _adaln_mod(c, w_ada[l], b_ada[l]).reshape(B, 6, D)
        w_in_b = w_in[l].astype(BF16)
        wqT = w_in_b[:, :A].T
        wk = w_in_b[:, A:2 * A]
        wvT = w_in_b[:, 2 * A:3 * A].T
        wu = w_in_b[:, 3 * A:]
        qT, k, vT, pool, sel = _mix_in(
            x, mod3, g_mix_pre[l][None, :], wqT, wk, wvT, wu, w_pool[l].astype(BF16),
            pool_scale[l][None, :], tables, tm=1024, sub_rows=512)
        attn = _moba_attn(qT, k, vT, sel, rows=2 if B % 2 == 0 else 1)
        x = _out_mlp(x, attn, pool, mod3, g_mix_post[l][None, :], g_mlp_pre[l][None, :],
                     g_mlp_post[l][None, :], w_out[l].astype(BF16), w_up[l].astype(BF16),
                     w_down[l].astype(BF16), tm=S // 2, sub_rows=MOBA_BLOCK, ff_chunk=1024)
    return x
```

```python
import functools

import jax
import jax.numpy as jnp
from jax import lax
from jax.experimental import pallas as pl
from jax.experimental.pallas import tpu as pltpu

F32 = jnp.float32
BF16 = jnp.bfloat16

HEAD_DIM = 64
HALF = HEAD_DIM // 2
MOBA_BLOCK = 256
MOBA_TOPK = 3
POOL_WINDOWS = (2, 4, 8, 16)
POOL_GROUP = 128
MAX_WINDOW = max(POOL_WINDOWS)
POOL_CHUNK = 128
ROPE_THETA = 10000.0
NORM_EPS = 1e-6
Q_SCALE = HEAD_DIM ** -0.5 * 1.4426950408889634
BF16_ROWS = 16
V_ROWS = HEAD_DIM + BF16_ROWS
LANES = 128
PAIR = LANES // HEAD_DIM
VMEM_LIMIT = 56 * 1024 * 1024

NT_DIMS = (((1,), (1,)), ((), ()))


def _rms(x):
    return x * lax.rsqrt(jnp.mean(x * x, axis=-1, keepdims=True) + NORM_EPS)


def _mod_kernel(c_ref, w_ref, b_ref, o_ref):
    c = c_ref[...]
    act = c * jax.nn.sigmoid(c)
    o_ref[...] = jnp.dot(act, w_ref[...], precision=lax.Precision.HIGHEST,
                         preferred_element_type=F32) + b_ref[...]


def _adaln_mod(c, w_ada, b_ada):
    B, D = c.shape
    n_out = w_ada.shape[1]
    return pl.pallas_call(
        _mod_kernel,
        grid=(n_out // D,),
        in_specs=[pl.BlockSpec((B, D), lambda n: (0, 0)),
                  pl.BlockSpec((D, D), lambda n: (0, n)),
                  pl.BlockSpec((1, D), lambda n: (0, n))],
        out_specs=pl.BlockSpec((B, D), lambda n: (0, n)),
        out_shape=jax.ShapeDtypeStruct((B, n_out), F32),
        name="adaln_mod",
    )(c, w_ada, b_ada.reshape(1, n_out))


def _fold_pool_kernel(wpool_ref, pscale_ref, wo_ref, o_ref):
    for g in range(wpool_ref.shape[0]):
        cols = slice(g * POOL_GROUP, (g + 1) * POOL_GROUP)
        o_ref[cols, :] = jnp.dot(wpool_ref[g] * pscale_ref[:, cols], wo_ref[cols, :],
                                 precision=lax.Precision.HIGHEST, preferred_element_type=F32)


def _fold_pool(w_pool, pool_scale, w_out, attn_width):
    n_groups, P, D = w_pool.shape[0], pool_scale.shape[1], w_out.shape[1]
    assert attn_width % P == 0 and n_groups * POOL_GROUP == P
    return pl.pallas_call(
        _fold_pool_kernel,
        grid=(1,),
        in_specs=[pl.BlockSpec((n_groups, POOL_GROUP, POOL_GROUP), lambda n: (0, 0, 0)),
                  pl.BlockSpec((1, P), lambda n: (0, 0)),
                  pl.BlockSpec((P, D), lambda n: (attn_width // P, 0))],
        out_specs=pl.BlockSpec((P, D), lambda n: (0, 0)),
        out_shape=jax.ShapeDtypeStruct((P, D), F32),
        name="fold_pool",
    )(w_pool, pool_scale, w_out)


def _mix_in_kernel(x_ref, mod_ref, g_ref, wqT_ref, wk_ref, wvT_ref, wu_ref,
                   cosn_ref, sina_ref, sinb_ref, cost_ref, sint_ref,
                   qT_ref, k_ref, vT_ref, pool_ref, sel_ref,
                   km_ref, ext_ref, *, tm, sub_rows, n_heads, n_blocks):
    i = pl.program_id(1)
    blocks_per_group = sub_rows // MOBA_BLOCK

    @pl.when(i == 0)
    def _():
        km_ref[...] = jnp.zeros_like(km_ref)
        ext_ref[:MAX_WINDOW, :] = jnp.zeros((MAX_WINDOW, ext_ref.shape[1]), F32)

    groups = [slice(r * sub_rows, (r + 1) * sub_rows) for r in range(tm // sub_rows)]
    gain = g_ref[...] * (1.0 + mod_ref[0, 1:2, :])
    hbs = []
    for rows in groups:
        x = x_ref[0, rows, :]
        hbs.append((_rms(x) * gain + mod_ref[0, 0:1, :]).astype(BF16))
    for gr, (rows, hb) in enumerate(zip(groups, hbs)):
        _mix_in_group(i * tm + gr * sub_rows, i * (tm // MOBA_BLOCK) + gr * blocks_per_group,
                      rows, gr * blocks_per_group, hb,
                      wqT_ref, wk_ref, wvT_ref, wu_ref,
                      cosn_ref, sina_ref, sinb_ref, cost_ref, sint_ref,
                      qT_ref, k_ref, vT_ref, pool_ref, sel_ref, km_ref, ext_ref,
                      n_heads=n_heads, n_blocks=n_blocks)


def _mix_in_group(pos0, blk0, rows, tile_blk0, hb,
                  wqT_ref, wk_ref, wvT_ref, wu_ref,
                  cosn_ref, sina_ref, sinb_ref, cost_ref, sint_ref,
                  qT_ref, k_ref, vT_ref, pool_ref, sel_ref, km_ref, ext_ref, *, n_heads, n_blocks):
    tg = hb.shape[0]
    blocks_per_group = tg // MOBA_BLOCK
    k = jnp.dot(hb, wk_ref[...], preferred_element_type=F32)
    u = jnp.dot(hb, wu_ref[...], preferred_element_type=F32)
    qT = lax.dot_general(wqT_ref[...], hb, NT_DIMS, preferred_element_type=F32)
    vT = lax.dot_general(wvT_ref[...], hb, NT_DIMS, preferred_element_type=F32)

    cosn, sina, sinb = cosn_ref[rows, :], sina_ref[rows, :], sinb_ref[rows, :]
    k_chunks = []
    for c in range(k.shape[1] // LANES):
        kc = k[:, c * LANES:(c + 1) * LANES]
        kr = (kc * cosn + pltpu.roll(kc, LANES - HALF, axis=1) * sina
              + pltpu.roll(kc, HALF, axis=1) * sinb)
        k_chunks.append(kr)
        k_ref[0, rows, c * LANES:(c + 1) * LANES] = kr.astype(BF16)

    for t in range(blocks_per_group):
        blk = slice(t * MOBA_BLOCK, (t + 1) * MOBA_BLOCK)
        mean = jnp.concatenate([jnp.sum(kr[blk], axis=0, keepdims=True) for kr in k_chunks],
                               axis=1) * (1.0 / MOBA_BLOCK)
        km_ref[pl.ds(blk0 + t, 1), :] = mean

    cos_t, sin_t = cost_ref[:, rows], sint_ref[:, rows]
    q_parts = []
    for hd in range(n_heads):
        x1 = qT[hd * HEAD_DIM:hd * HEAD_DIM + HALF]
        x2 = qT[hd * HEAD_DIM + HALF:(hd + 1) * HEAD_DIM]
        q_parts.append(x1 * cos_t - x2 * sin_t)
        q_parts.append(x2 * cos_t + x1 * sin_t)
    qs = jnp.concatenate(q_parts, axis=0) * Q_SCALE
    qs_hi = qs.astype(BF16)
    qT_ref[0, :, rows] = qs_hi
    ones_rows = jnp.where(lax.broadcasted_iota(jnp.int32, (V_ROWS - HEAD_DIM, tg), 0) == 0, 1.0, 0.0)
    v_parts = []
    for hd in range(n_heads):
        v_parts += [vT[hd * HEAD_DIM:(hd + 1) * HEAD_DIM], ones_rows]
    v_aug = jnp.concatenate(v_parts, axis=0).astype(BF16)
    for t in range(blocks_per_group):
        vT_ref[0, tile_blk0 + t] = v_aug[:, t * MOBA_BLOCK:(t + 1) * MOBA_BLOCK]

    km = km_ref[...]
    km_rep = jnp.concatenate([km] * n_heads, axis=0)
    r_head = lax.broadcasted_iota(jnp.int32, km_rep.shape, 0) // n_blocks
    l_head = lax.broadcasted_iota(jnp.int32, km_rep.shape, 1) // HEAD_DIM
    km_bd = jnp.where(r_head == l_head, km_rep, 0.0)
    km_hi = km_bd.astype(BF16)
    km_lo = (km_bd - km_hi.astype(F32)).astype(BF16)
    qs_lo = (qs - qs_hi.astype(F32)).astype(BF16)
    gate = (jnp.dot(km_hi, qs_hi, preferred_element_type=F32)
            + jnp.dot(km_hi, qs_lo, preferred_element_type=F32)
            + jnp.dot(km_lo, qs_hi, preferred_element_type=F32))

    q_blk = (pos0 + lax.broadcasted_iota(jnp.int32, (n_blocks, tg), 1)) // MOBA_BLOCK
    row_j = lax.broadcasted_iota(jnp.int32, (n_blocks, tg), 0)
    for hd in range(n_heads):
        g = gate[hd * n_blocks:(hd + 1) * n_blocks]
        rank = jnp.zeros((n_blocks, tg), jnp.int32)
        for j in range(n_blocks):
            gj = jnp.broadcast_to(g[j:j + 1, :], g.shape)
            ahead = (gj > g) | ((gj == g) & (j < row_j))
            rank = rank + jnp.where(ahead & (j < q_blk), 1, 0)
        chosen = (row_j < q_blk) & (rank < MOBA_TOPK)
        sel_ref[0, hd * n_blocks:(hd + 1) * n_blocks, rows] = jnp.where(chosen, 0.0, -jnp.inf)

    ext_ref[MAX_WINDOW:, :] = u
    for r0 in range(0, tg, POOL_CHUNK):
        pos1 = pos0 + r0 + lax.broadcasted_iota(jnp.int32, (POOL_CHUNK, 1), 0) + 1
        for gi, w in enumerate(POOL_WINDOWS):
            cols = slice(gi * POOL_GROUP, (gi + 1) * POOL_GROUP)
            win = ext_ref[r0:r0 + MAX_WINDOW + POOL_CHUNK, cols]
            token = win[MAX_WINDOW:, :]
            span = 1
            while span < w:
                win = win + pltpu.roll(win, span, axis=0)
                span *= 2
            count = jnp.minimum(pos1, w).astype(F32)
            pool_ref[0, rows.start + r0:rows.start + r0 + POOL_CHUNK, cols] = (
                win[MAX_WINDOW:, :] / count - token).astype(BF16)
    ext_ref[:MAX_WINDOW, :] = ext_ref[tg:, :]


def _mix_in(x, mod3, g_pre, wqT, wk, wvT, wu, tables, *, tm, sub_rows):
    B, S, D = x.shape
    A = wk.shape[1]
    P = wu.shape[1]
    n_heads = A // HEAD_DIM
    n_blocks = S // MOBA_BLOCK
    bpt = tm // MOBA_BLOCK
    cosn, sina, sinb, cost, sint = tables
    const = lambda *shape: pl.BlockSpec(shape, lambda b, i: (0,) * len(shape),
                                        pipeline_mode=pl.Buffered(1))
    kern = functools.partial(_mix_in_kernel, tm=tm, sub_rows=sub_rows, n_heads=n_heads,
                             n_blocks=n_blocks)
    return pl.pallas_call(
        kern,
        grid=(B, S // tm),
        in_specs=[
            pl.BlockSpec((1, tm, D), lambda b, i: (b, i, 0)),
            pl.BlockSpec((1, 6, D), lambda b, i: (b, 0, 0)),
            const(1, D),
            const(A, D), const(D, A), const(A, D), const(D, P),
            pl.BlockSpec((tm, LANES), lambda b, i: (i, 0)),
            pl.BlockSpec((tm, LANES), lambda b, i: (i, 0)),
            pl.BlockSpec((tm, LANES), lambda b, i: (i, 0)),
            pl.BlockSpec((HALF, tm), lambda b, i: (0, i)),
            pl.BlockSpec((HALF, tm), lambda b, i: (0, i)),
        ],
        out_specs=[
            pl.BlockSpec((1, A, tm), lambda b, i: (b, 0, i)),
            pl.BlockSpec((1, tm, A), lambda b, i: (b, i, 0)),
            pl.BlockSpec((1, bpt, n_heads * V_ROWS, MOBA_BLOCK), lambda b, i: (b, i, 0, 0)),
            pl.BlockSpec((1, tm, P), lambda b, i: (b, i, 0)),
            pl.BlockSpec((1, n_heads * n_blocks, tm), lambda b, i: (b, 0, i)),
        ],
        out_shape=[
            jax.ShapeDtypeStruct((B, A, S), BF16),
            jax.ShapeDtypeStruct((B, S, A), BF16),
            jax.ShapeDtypeStruct((B, n_blocks, n_heads * V_ROWS, MOBA_BLOCK), BF16),
            jax.ShapeDtypeStruct((B, S, P), BF16),
            jax.ShapeDtypeStruct((B, n_heads * n_blocks, S), F32),
        ],
        scratch_shapes=[pltpu.VMEM((n_blocks, A), F32),
                        pltpu.VMEM((MAX_WINDOW + sub_rows, P), F32)],
        compiler_params=pltpu.CompilerParams(dimension_semantics=("parallel", "arbitrary"),
                                             vmem_limit_bytes=VMEM_LIMIT),
        name="mix_in",
    )(x, mod3, g_pre, wqT, wk, wvT, wu, cosn, sina, sinb, cost, sint)


def _attn_kernel(qa_ref, qb_ref, k_ref, vT_ref, sela_ref, selb_ref, o_ref,
                 qz_ref, sel_ref, m_ref, acc_ref,
                 s0_ref, alpha0_ref, shift0_ref, s1_ref, alpha1_ref, shift1_ref,
                 *, rows, n_heads, n_blocks):
    p = pl.program_id(1)
    q_blocks = (p, n_blocks - 1 - p)
    tq = MOBA_BLOCK
    items = [(r, hd) for r in range(rows) for hd in range(n_heads)]
    n_items = len(items)
    halves = (list(enumerate(items))[:n_items // 2], list(enumerate(items))[n_items // 2:])
    zeros = jnp.zeros((HEAD_DIM, tq), BF16)
    for w, q_ref in enumerate((qa_ref, qb_ref)):
        for it, (r, hd) in enumerate(items):
            qh = q_ref[r, hd * HEAD_DIM:(hd + 1) * HEAD_DIM, :]
            qz_ref[w * n_items + it] = jnp.concatenate(
                [qh, zeros] if hd % PAIR == 0 else [zeros, qh], axis=0)
    sel_ref[0] = sela_ref[...]
    sel_ref[1] = selb_ref[...]
    acc_ref[...] = jnp.zeros_like(acc_ref)
    buffers = ((s0_ref, alpha0_ref, shift0_ref), (s1_ref, alpha1_ref, shift1_ref))

    def pass1(half, w, j, own, dst):
        s_ref, alpha_ref, shift_ref = dst
        keys = pl.ds(pl.multiple_of(j * tq, tq), tq)
        qz_w, m_w = qz_ref.at[pl.ds(w * n_items, n_items)], m_ref.at[pl.ds(w * n_items, n_items)]
        for it, (r, hd) in halves[half]:
            c = hd // PAIR
            s = jnp.dot(k_ref[r, keys, c * LANES:(c + 1) * LANES], qz_w[it],
                        preferred_element_type=F32)
            if own:
                key_i = lax.broadcasted_iota(jnp.int32, (tq, tq), 0)
                qry_i = lax.broadcasted_iota(jnp.int32, (tq, tq), 1)
                s = jnp.where(key_i <= qry_i, s, -jnp.inf)
            s_ref[it] = s
            cmax = jnp.max(s, axis=0, keepdims=True)
            if own:
                m_w[it] = cmax
                alpha_ref[it] = jnp.zeros_like(cmax)
                shift_ref[it] = cmax
            else:
                bias = sel_ref[w, r, pl.ds(hd * n_blocks + j, 1), :]
                m_old = m_w[it]
                m_new = jnp.maximum(m_old, cmax + bias)
                m_w[it] = m_new
                alpha_ref[it] = jnp.exp2(m_old - m_new)
                shift_ref[it] = m_new - bias

    def pass2(half, w, j, src):
        s_ref, alpha_ref, shift_ref = src
        acc_w = acc_ref.at[pl.ds(w * n_items, n_items)]
        for it, (r, hd) in halves[half]:
            prob = jnp.exp2((s_ref[it] - shift_ref[it]).astype(BF16))
            pv = jnp.dot(vT_ref[r, j, hd * V_ROWS:(hd + 1) * V_ROWS, :], prob,
                         preferred_element_type=F32)
            acc_w[it] = alpha_ref[it] * acc_w[it] + pv

    n_elems = n_blocks + 1

    def elem(n):
        if isinstance(n, int) and n < 2:
            return n, q_blocks[n], True
        e = n - 2
        first = e < q_blocks[0]
        return jnp.where(first, 0, 1), jnp.where(first, e, e - q_blocks[0]), False

    def turn(t, parity):
        for half, n in ((0, t), (1, t - 1)):
            static = isinstance(n, int)
            if not static or 1 <= n <= n_elems:
                w, j, _ = elem(n - 1)
                pass2(half, w, j, buffers[(parity + 1 + half) % 2])
            if not static or 0 <= n < n_elems:
                w, j, own = elem(n)
                pass1(half, w, j, own, buffers[(parity + half) % 2])

    first_rolled, n_rolled = 4, 2 * ((n_elems - 4) // 2)
    for t in range(first_rolled):
        turn(t, t % 2)

    def two_turns(u, carry):
        t = first_rolled + 2 * u
        turn(t, first_rolled % 2)
        turn(t + 1, (first_rolled + 1) % 2)
        return carry

    lax.fori_loop(0, n_rolled // 2, two_turns, 0)
    for t in range(first_rolled + n_rolled, n_elems + 2):
        turn(t, t % 2)

    for w in range(2):
        for r in range(rows):
            for c in range(n_heads // PAIR):
                pair = [acc_ref[it, :HEAD_DIM] * (1.0 / acc_ref[it, HEAD_DIM:HEAD_DIM + 1])
                        for it in range(w * n_items + r * n_heads + c * PAIR,
                                        w * n_items + r * n_heads + (c + 1) * PAIR)]
                o_ref[r, w, 0, :, c * LANES:(c + 1) * LANES] = (
                    jnp.concatenate(pair, axis=0).T.astype(BF16))


def _moba_attn(qT, k, vT, sel, *, rows):
    B, A, S = qT.shape
    n_heads = A // HEAD_DIM
    n_blocks = S // MOBA_BLOCK
    assert n_blocks % 2 == 0 and (n_blocks - 1) % 2 == 1
    n_items = rows * n_heads
    kern = functools.partial(_attn_kernel, rows=rows, n_heads=n_heads, n_blocks=n_blocks)
    stage = [pltpu.VMEM((n_items, MOBA_BLOCK, MOBA_BLOCK), F32),
             pltpu.VMEM((n_items, 1, MOBA_BLOCK), F32),
             pltpu.VMEM((n_items, 1, MOBA_BLOCK), F32)]
    last = n_blocks - 1
    return pl.pallas_call(
        kern,
        grid=(B // rows, n_blocks // 2),
        in_specs=[
            pl.BlockSpec((rows, A, MOBA_BLOCK), lambda b, p: (b, 0, p)),
            pl.BlockSpec((rows, A, MOBA_BLOCK), lambda b, p: (b, 0, last - p)),
            pl.BlockSpec((rows, S, A), lambda b, p: (b, 0, 0)),
            pl.BlockSpec((rows, n_blocks, n_heads * V_ROWS, MOBA_BLOCK), lambda b, p: (b, 0, 0, 0)),
            pl.BlockSpec((rows, n_heads * n_blocks, MOBA_BLOCK), lambda b, p: (b, 0, p)),
            pl.BlockSpec((rows, n_heads * n_blocks, MOBA_BLOCK), lambda b, p: (b, 0, last - p)),
        ],
        out_specs=pl.BlockSpec((rows, 2, 1, MOBA_BLOCK, A), lambda b, p: (b, 0, p, 0, 0)),
        out_shape=jax.ShapeDtypeStruct((B, 2, n_blocks // 2, MOBA_BLOCK, A), BF16),
        scratch_shapes=[
            pltpu.VMEM((2 * n_items, LANES, MOBA_BLOCK), BF16),
            pltpu.VMEM((2, rows, n_heads * n_blocks, MOBA_BLOCK), F32),
            pltpu.VMEM((2 * n_items, 1, MOBA_BLOCK), F32),
            pltpu.VMEM((2 * n_items, V_ROWS, MOBA_BLOCK), F32),
        ] + stage + stage,
        compiler_params=pltpu.CompilerParams(dimension_semantics=("parallel", "arbitrary"),
                                             vmem_limit_bytes=VMEM_LIMIT),
        name="moba_attn",
    )(qT, qT, k, vT, sel, sel)


def _out_mlp_kernel(x_ref, attn_ref, pool_ref, mod_ref, g_post_ref, g_pre2_ref, g_post2_ref,
                    wo_ref, wup_ref, wdn_ref, o_ref, *, sub_rows, ff_chunk):
    A = attn_ref.shape[4]
    n_groups = x_ref.shape[1] // sub_rows
    half = pl.program_id(1)
    groups = [slice(r * sub_rows, (r + 1) * sub_rows) for r in range(n_groups)]
    ys = [jnp.dot(attn_ref[0, 0, jnp.where(half == 0, r, n_groups - 1 - r)], wo_ref[:A, :],
                  preferred_element_type=F32)
          + jnp.dot(pool_ref[0, rows, :], wo_ref[A:, :], preferred_element_type=F32)
          for r, rows in enumerate(groups)]
    gain1 = mod_ref[0, 2:3, :] * g_post_ref[...]
    gain2 = g_pre2_ref[...] * (1.0 + mod_ref[0, 4:5, :])
    gain3 = mod_ref[0, 5:6, :] * g_post2_ref[...]
    for rows, y in zip(groups, ys):
        x1 = x_ref[0, rows, :] + _rms(y) * gain1
        hb = (_rms(x1) * gain2 + mod_ref[0, 3:4, :]).astype(BF16)
        y2 = jnp.zeros_like(x1)
        for c in range(wup_ref.shape[1] // ff_chunk):
            cols = slice(c * ff_chunk, (c + 1) * ff_chunk)
            up = jnp.dot(hb, wup_ref[:, cols], preferred_element_type=F32)
            act = jnp.square(jnp.maximum(up, 0.0)).astype(BF16)
            y2 = y2 + jnp.dot(act, wdn_ref[cols, :], preferred_element_type=F32)
        o_ref[0, rows, :] = x1 + _rms(y2) * gain3


def _out_mlp(x, attn, pool, mod3, g_post, g_pre2, g_post2, wo, wup, wdn, *, tm, sub_rows,
             ff_chunk):
    B, S, D = x.shape
    A, P, FF = attn.shape[4], pool.shape[2], wup.shape[1]
    assert attn.shape[1:4] == (S // tm, tm // sub_rows, sub_rows)
    const = lambda *shape: pl.BlockSpec(shape, lambda b, i: (0,) * len(shape),
                                        pipeline_mode=pl.Buffered(1))
    return pl.pallas_call(
        functools.partial(_out_mlp_kernel, sub_rows=sub_rows, ff_chunk=ff_chunk),
        grid=(B, S // tm),
        in_specs=[
            pl.BlockSpec((1, tm, D), lambda b, i: (b, i, 0)),
            pl.BlockSpec((1, 1) + attn.shape[2:], lambda b, i: (b, i, 0, 0, 0)),
            pl.BlockSpec((1, tm, P), lambda b, i: (b, i, 0)),
            pl.BlockSpec((1, 6, D), lambda b, i: (b, 0, 0)),
            const(1, D), const(1, D), const(1, D),
            const(A + P, D), const(D, FF), const(FF, D),
        ],
        out_specs=pl.BlockSpec((1, tm, D), lambda b, i: (b, i, 0)),
        out_shape=jax.ShapeDtypeStruct((B, S, D), F32),
        compiler_params=pltpu.CompilerParams(dimension_semantics=("parallel", "parallel"),
                                             vmem_limit_bytes=VMEM_LIMIT),
        name="out_mlp",
    )(x, attn, pool, mod3, g_post, g_pre2, g_post2, wo, wup, wdn)


def _rope_tables(S):
    inv_freq = 1.0 / (ROPE_THETA ** (jnp.arange(HALF, dtype=F32) * (2.0 / HEAD_DIM)))
    ang = jnp.arange(S, dtype=jnp.int32).astype(F32)[:, None] * inv_freq[None, :]
    cos, sin = jnp.cos(ang), jnp.sin(ang)
    zero = jnp.zeros_like(sin)
    reps = LANES // HEAD_DIM
    cosn = jnp.tile(jnp.concatenate([cos, cos], axis=1), (1, reps))
    sina = jnp.tile(jnp.concatenate([-sin, zero], axis=1), (1, reps))
    sinb = jnp.tile(jnp.concatenate([zero, sin], axis=1), (1, reps))
    return cosn, sina, sinb, cos.T, sin.T


def kernel(x, c, w_ada, b_ada, g_mix_pre, g_mix_post, w_in, w_pool, pool_scale, w_out,
           g_mlp_pre, g_mlp_post, w_up, w_down):
    B, S, D = x.shape
    depth = w_ada.shape[0]
    P = pool_scale.shape[1]
    A = w_out.shape[1] - P
    tables = _rope_tables(S)
    for l in range(depth):
        mod3 = _adaln_mod(c, w_ada[l], b_ada[l]).reshape(B, 6, D)
        w_in_b = w_in[l].astype(BF16)
        wqT = w_in_b[:, :A].T
        wk = w_in_b[:, A:2 * A]
        wvT = w_in_b[:, 2 * A:3 * A].T
        wu = w_in_b[:, 3 * A:]
        qT, k, vT, pool_in, sel = _mix_in(x, mod3, g_mix_pre[l][None, :], wqT, wk, wvT, wu, tables,
                                          tm=1024, sub_rows=256)
        wo = jnp.concatenate([w_out[l][:A], _fold_pool(w_pool[l], pool_scale[l][None, :],
                                                      w_out[l], A)], axis=0).astype(BF16)
        attn = _moba_attn(qT, k, vT, sel, rows=2 if B % 2 == 0 else 1)
        x = _out_mlp(x, attn, pool_in, mod3, g_mix_post[l][None, :], g_mlp_pre[l][None, :],
                     g_mlp_post[l][None, :], wo, w_up[l].astype(BF16),
                     w_down[l].astype(BF16), tm=S // 2, sub_rows=MOBA_BLOCK, ff_chunk=1024)
    return x
```

```python
import functools

import jax
import jax.numpy as jnp
from jax import lax
from jax.experimental import pallas as pl
from jax.experimental.pallas import tpu as pltpu

F32 = jnp.float32
BF16 = jnp.bfloat16

HEAD_DIM = 64
HALF = HEAD_DIM // 2
MOBA_BLOCK = 256
MOBA_TOPK = 3
POOL_WINDOWS = (2, 4, 8, 16)
POOL_GROUP = 128
MAX_WINDOW = max(POOL_WINDOWS)
POOL_CHUNK = 128
ROPE_THETA = 10000.0
NORM_EPS = 1e-6
Q_SCALE = HEAD_DIM ** -0.5 * 1.4426950408889634
BF16_ROWS = 16
V_ROWS = HEAD_DIM + BF16_ROWS
LANES = 128
PAIR = LANES // HEAD_DIM
VMEM_LIMIT = 56 * 1024 * 1024

NT_DIMS = (((1,), (1,)), ((), ()))


def _rms(x):
    return x * lax.rsqrt(jnp.mean(x * x, axis=-1, keepdims=True) + NORM_EPS)


def _mod_kernel(c_ref, w_ref, b_ref, o_ref):
    c = c_ref[...]
    act = c * jax.nn.sigmoid(c)
    o_ref[...] = jnp.dot(act, w_ref[...], precision=lax.Precision.HIGHEST,
                         preferred_element_type=F32) + b_ref[...]


def _adaln_mod(c, w_ada, b_ada):
    B, D = c.shape
    n_out = w_ada.shape[1]
    return pl.pallas_call(
        _mod_kernel,
        grid=(n_out // D,),
        in_specs=[pl.BlockSpec((B, D), lambda n: (0, 0)),
                  pl.BlockSpec((D, D), lambda n: (0, n)),
                  pl.BlockSpec((1, D), lambda n: (0, n))],
        out_specs=pl.BlockSpec((B, D), lambda n: (0, n)),
        out_shape=jax.ShapeDtypeStruct((B, n_out), F32),
        name="adaln_mod",
    )(c, w_ada, b_ada.reshape(1, n_out))


def _fold_pool_kernel(wpool_ref, pscale_ref, wo_ref, o_ref):
    for g in range(wpool_ref.shape[0]):
        cols = slice(g * POOL_GROUP, (g + 1) * POOL_GROUP)
        o_ref[cols, :] = jnp.dot(wpool_ref[g] * pscale_ref[:, cols], wo_ref[cols, :],
                                 precision=lax.Precision.HIGHEST, preferred_element_type=F32)


def _fold_pool(w_pool, pool_scale, w_out, attn_width):
    n_groups, P, D = w_pool.shape[0], pool_scale.shape[1], w_out.shape[1]
    assert attn_width % P == 0 and n_groups * POOL_GROUP == P
    return pl.pallas_call(
        _fold_pool_kernel,
        grid=(1,),
        in_specs=[pl.BlockSpec((n_groups, POOL_GROUP, POOL_GROUP), lambda n: (0, 0, 0)),
                  pl.BlockSpec((1, P), lambda n: (0, 0)),
                  pl.BlockSpec((P, D), lambda n: (attn_width // P, 0))],
        out_specs=pl.BlockSpec((P, D), lambda n: (0, 0)),
        out_shape=jax.ShapeDtypeStruct((P, D), F32),
        name="fold_pool",
    )(w_pool, pool_scale, w_out)


def _mix_in_kernel(x_ref, mod_ref, g_ref, wqT_ref, wk_ref, wvT_ref, wu_ref,
                   cosn_ref, sina_ref, sinb_ref, cost_ref, sint_ref,
                   qT_ref, k_ref, vT_ref, pool_ref, km_ref,
                   ext_ref, *, tm, sub_rows, n_heads):
    i = pl.program_id(1)
    blocks_per_group = sub_rows // MOBA_BLOCK

    @pl.when(i == 0)
    def _():
        ext_ref[:MAX_WINDOW, :] = jnp.zeros((MAX_WINDOW, ext_ref.shape[1]), F32)

    groups = [slice(r * sub_rows, (r + 1) * sub_rows) for r in range(tm // sub_rows)]
    gain = g_ref[...] * (1.0 + mod_ref[0, 1:2, :])
    hbs = []
    for rows in groups:
        x = x_ref[0, rows, :]
        hbs.append((_rms(x) * gain + mod_ref[0, 0:1, :]).astype(BF16))
    for gr, (rows, hb) in enumerate(zip(groups, hbs)):
        _mix_in_group(i * tm + gr * sub_rows, i * (tm // MOBA_BLOCK) + gr * blocks_per_group,
                      rows, gr * blocks_per_group, hb,
                      wqT_ref, wk_ref, wvT_ref, wu_ref,
                      cosn_ref, sina_ref, sinb_ref, cost_ref, sint_ref,
                      qT_ref, k_ref, vT_ref, pool_ref, km_ref, ext_ref, n_heads=n_heads)


def _mix_in_group(pos0, blk0, rows, tile_blk0, hb,
                  wqT_ref, wk_ref, wvT_ref, wu_ref,
                  cosn_ref, sina_ref, sinb_ref, cost_ref, sint_ref,
                  qT_ref, k_ref, vT_ref, pool_ref, km_ref, ext_ref, *, n_heads):
    tg = hb.shape[0]
    blocks_per_group = tg // MOBA_BLOCK
    k = jnp.dot(hb, wk_ref[...], preferred_element_type=F32)
    u = jnp.dot(hb, wu_ref[...], preferred_element_type=F32)
    qT = lax.dot_general(wqT_ref[...], hb, NT_DIMS, preferred_element_type=F32)
    vT = lax.dot_general(wvT_ref[...], hb, NT_DIMS, preferred_element_type=F32)

    cosn, sina, sinb = cosn_ref[rows, :], sina_ref[rows, :], sinb_ref[rows, :]
    k_chunks = []
    for c in range(k.shape[1] // LANES):
        kc = k[:, c * LANES:(c + 1) * LANES]
        kr = (kc * cosn + pltpu.roll(kc, LANES - HALF, axis=1) * sina
              + pltpu.roll(kc, HALF, axis=1) * sinb)
        k_chunks.append(kr)
        k_ref[0, rows, c * LANES:(c + 1) * LANES] = kr.astype(BF16)

    for t in range(blocks_per_group):
        blk = slice(t * MOBA_BLOCK, (t + 1) * MOBA_BLOCK)
        mean = jnp.concatenate([jnp.sum(kr[blk], axis=0, keepdims=True) for kr in k_chunks],
                               axis=1) * (1.0 / MOBA_BLOCK)
        km_ref[0, pl.ds(blk0 + t, 1), :] = mean

    cos_t, sin_t = cost_ref[:, rows], sint_ref[:, rows]
    q_parts = []
    for hd in range(n_heads):
        x1 = qT[hd * HEAD_DIM:hd * HEAD_DIM + HALF]
        x2 = qT[hd * HEAD_DIM + HALF:(hd + 1) * HEAD_DIM]
        q_parts.append(x1 * cos_t - x2 * sin_t)
        q_parts.append(x2 * cos_t + x1 * sin_t)
    qT_ref[0, :, rows] = (jnp.concatenate(q_parts, axis=0) * Q_SCALE).astype(BF16)
    ones_rows = jnp.where(lax.broadcasted_iota(jnp.int32, (V_ROWS - HEAD_DIM, tg), 0) == 0, 1.0, 0.0)
    v_parts = []
    for hd in range(n_heads):
        v_parts += [vT[hd * HEAD_DIM:(hd + 1) * HEAD_DIM], ones_rows]
    v_aug = jnp.concatenate(v_parts, axis=0).astype(BF16)
    for t in range(blocks_per_group):
        vT_ref[0, tile_blk0 + t] = v_aug[:, t * MOBA_BLOCK:(t + 1) * MOBA_BLOCK]

    ext_ref[MAX_WINDOW:, :] = u
    for r0 in range(0, tg, POOL_CHUNK):
        pos1 = pos0 + r0 + lax.broadcasted_iota(jnp.int32, (POOL_CHUNK, 1), 0) + 1
        for gi, w in enumerate(POOL_WINDOWS):
            cols = slice(gi * POOL_GROUP, (gi + 1) * POOL_GROUP)
            win = ext_ref[r0:r0 + MAX_WINDOW + POOL_CHUNK, cols]
            token = win[MAX_WINDOW:, :]
            span = 1
            while span < w:
                win = win + pltpu.roll(win, span, axis=0)
                span *= 2
            count = jnp.minimum(pos1, w).astype(F32)
            pool_ref[0, rows.start + r0:rows.start + r0 + POOL_CHUNK, cols] = (
                win[MAX_WINDOW:, :] / count - token).astype(BF16)
    ext_ref[:MAX_WINDOW, :] = ext_ref[tg:, :]


def _mix_in(x, mod3, g_pre, wqT, wk, wvT, wu, tables, *, tm, sub_rows):
    B, S, D = x.shape
    A = wk.shape[1]
    P = wu.shape[1]
    n_heads = A // HEAD_DIM
    n_blocks = S // MOBA_BLOCK
    bpt = tm // MOBA_BLOCK
    cosn, sina, sinb, cost, sint = tables
    const = lambda *shape: pl.BlockSpec(shape, lambda b, i: (0,) * len(shape),
                                        pipeline_mode=pl.Buffered(1))
    kern = functools.partial(_mix_in_kernel, tm=tm, sub_rows=sub_rows, n_heads=n_heads)
    return pl.pallas_call(
        kern,
        grid=(B, S // tm),
        in_specs=[
            pl.BlockSpec((1, tm, D), lambda b, i: (b, i, 0)),
            pl.BlockSpec((1, 6, D), lambda b, i: (b, 0, 0)),
            const(1, D),
            const(A, D), const(D, A), const(A, D), const(D, P),
            pl.BlockSpec((tm, LANES), lambda b, i: (i, 0)),
            pl.BlockSpec((tm, LANES), lambda b, i: (i, 0)),
            pl.BlockSpec((tm, LANES), lambda b, i: (i, 0)),
            pl.BlockSpec((HALF, tm), lambda b, i: (0, i)),
            pl.BlockSpec((HALF, tm), lambda b, i: (0, i)),
        ],
        out_specs=[
            pl.BlockSpec((1, A, tm), lambda b, i: (b, 0, i)),
            pl.BlockSpec((1, tm, A), lambda b, i: (b, i, 0)),
            pl.BlockSpec((1, bpt, n_heads * V_ROWS, MOBA_BLOCK), lambda b, i: (b, i, 0, 0)),
            pl.BlockSpec((1, tm, P), lambda b, i: (b, i, 0)),
            pl.BlockSpec((1, n_blocks, A), lambda b, i: (b, 0, 0)),
        ],
        out_shape=[
            jax.ShapeDtypeStruct((B, A, S), BF16),
            jax.ShapeDtypeStruct((B, S, A), BF16),
            jax.ShapeDtypeStruct((B, n_blocks, n_heads * V_ROWS, MOBA_BLOCK), BF16),
            jax.ShapeDtypeStruct((B, S, P), BF16),
            jax.ShapeDtypeStruct((B, n_blocks, A), F32),
        ],
        scratch_shapes=[pltpu.VMEM((MAX_WINDOW + sub_rows, P), F32)],
        compiler_params=pltpu.CompilerParams(dimension_semantics=("parallel", "arbitrary"),
                                             vmem_limit_bytes=VMEM_LIMIT),
        name="mix_in",
    )(x, mod3, g_pre, wqT, wk, wvT, wu, cosn, sina, sinb, cost, sint)


def _attn_kernel(qa_ref, qb_ref, k_ref, vT_ref, km_ref, o_ref,
                 qz_ref, kmx_ref, sel_ref, m_ref, acc_ref,
                 s0_ref, alpha0_ref, shift0_ref, s1_ref, alpha1_ref, shift1_ref,
                 *, rows, n_heads, n_blocks):
    p = pl.program_id(1)
    q_blocks = (p, n_blocks - 1 - p)
    tq = MOBA_BLOCK
    items = [(r, hd) for r in range(rows) for hd in range(n_heads)]
    n_items = len(items)
    n_chunks = n_heads // PAIR
    halves = (list(enumerate(items))[:n_items // 2], list(enumerate(items))[n_items // 2:])
    zeros = jnp.zeros((HEAD_DIM, tq), BF16)
    for w, q_ref in enumerate((qa_ref, qb_ref)):
        for it, (r, hd) in enumerate(items):
            qh = q_ref[r, hd * HEAD_DIM:(hd + 1) * HEAD_DIM, :]
            qz_ref[w * n_items + it] = jnp.concatenate(
                [qh, zeros] if hd % PAIR == 0 else [zeros, qh], axis=0)
    for r in range(rows):
        for c in range(n_chunks):
            km = km_ref[r, :, c * LANES:(c + 1) * LANES]
            km_hi = km.astype(BF16).astype(F32)
            kmx_ref[r * n_chunks + c] = jnp.concatenate([km_hi, km - km_hi], axis=0).astype(BF16)
    acc_ref[...] = jnp.zeros_like(acc_ref)
    buffers = ((s0_ref, alpha0_ref, shift0_ref), (s1_ref, alpha1_ref, shift1_ref))
    row_j = lax.broadcasted_iota(jnp.int32, (n_blocks, tq), 0)

    def select_blocks(gate, n_valid):
        valid = row_j < n_valid
        g = jnp.where(valid, gate, -jnp.inf)
        rank = jnp.zeros((n_blocks, tq), jnp.int32)
        for j in range(n_blocks):
            gj = jnp.broadcast_to(g[j:j + 1, :], g.shape)
            ahead = (gj > g) | ((gj == g) & (row_j > j))
            rank = rank + jnp.where(ahead, 1, 0)
        return jnp.where(valid & (rank < MOBA_TOPK), 0.0, -jnp.inf)

    def pass1(half, w, j, own, dst):
        s_ref, alpha_ref, shift_ref = dst
        keys = pl.ds(pl.multiple_of(j * tq, tq), tq)
        qz_w, m_w = qz_ref.at[pl.ds(w * n_items, n_items)], m_ref.at[pl.ds(w * n_items, n_items)]
        for it, (r, hd) in halves[half]:
            c = hd // PAIR
            kp = k_ref[r, keys, c * LANES:(c + 1) * LANES]
            if own:
                kp = jnp.concatenate([kp, kmx_ref[r * n_chunks + c]], axis=0)
            s = jnp.dot(kp, qz_w[it], preferred_element_type=F32)
            if own:
                gate = s[tq:tq + n_blocks] + s[tq + n_blocks:]
                sel_ref[w, r, hd * n_blocks:(hd + 1) * n_blocks, :] = select_blocks(gate, j)
                key_i = lax.broadcasted_iota(jnp.int32, (tq, tq), 0)
                qry_i = lax.broadcasted_iota(jnp.int32, (tq, tq), 1)
                s = jnp.where(key_i <= qry_i, s[:tq], -jnp.inf)
            s_ref[it] = s
            cmax = jnp.max(s, axis=0, keepdims=True)
            if own:
                m_w[it] = cmax
                alpha_ref[it] = jnp.zeros_like(cmax)
                shift_ref[it] = cmax
            else:
                bias = sel_ref[w, r, pl.ds(hd * n_blocks + j, 1), :]
                m_old = m_w[it]
                m_new = jnp.maximum(m_old, cmax + bias)
                m_w[it] = m_new
                alpha_ref[it] = jnp.exp2(m_old - m_new)
                shift_ref[it] = m_new - bias

    def pass2(half, w, j, src):
        s_ref, alpha_ref, shift_ref = src
        acc_w = acc_ref.at[pl.ds(w * n_items, n_items)]
        for it, (r, hd) in halves[half]:
            prob = jnp.exp2((s_ref[it] - shift_ref[it]).astype(BF16))
            pv = jnp.dot(vT_ref[r, j, hd * V_ROWS:(hd + 1) * V_ROWS, :], prob,
                         preferred_element_type=F32)
            acc_w[it] = alpha_ref[it] * acc_w[it] + pv

    n_elems = n_blocks + 1

    def elem(n):
        if isinstance(n, int) and n < 2:
            return n, q_blocks[n], True
        e = n - 2
        first = e < q_blocks[0]
        return jnp.where(first, 0, 1), jnp.where(first, e, e - q_blocks[0]), False

    def turn(t, parity):
        for half, n in ((0, t), (1, t - 1)):
            static = isinstance(n, int)
            if not static or 1 <= n <= n_elems:
                w, j, _ = elem(n - 1)
                pass2(half, w, j, buffers[(parity + 1 + half) % 2])
            if not static or 0 <= n < n_elems:
                w, j, own = elem(n)
                pass1(half, w, j, own, buffers[(parity + half) % 2])

    first_rolled, n_rolled = 4, 2 * ((n_elems - 4) // 2)
    for t in range(first_rolled):
        turn(t, t % 2)

    def two_turns(u, carry):
        t = first_rolled + 2 * u
        turn(t, first_rolled % 2)
        turn(t + 1, (first_rolled + 1) % 2)
        return carry

    lax.fori_loop(0, n_rolled // 2, two_turns, 0)
    for t in range(first_rolled + n_rolled, n_elems + 2):
        turn(t, t % 2)

    for w in range(2):
        for r in range(rows):
            for c in range(n_heads // PAIR):
                pair = [acc_ref[it, :HEAD_DIM] * (1.0 / acc_ref[it, HEAD_DIM:HEAD_DIM + 1])
                        for it in range(w * n_items + r * n_heads + c * PAIR,
                                        w * n_items + r * n_heads + (c + 1) * PAIR)]
                o_ref[r, w, 0, :, c * LANES:(c + 1) * LANES] = (
                    jnp.concatenate(pair, axis=0).T.astype(BF16))


def _moba_attn(qT, k, vT, km, *, rows):
    B, A, S = qT.shape
    n_heads = A // HEAD_DIM
    n_blocks = S // MOBA_BLOCK
    assert n_blocks % 2 == 0 and (n_blocks - 1) % 2 == 1
    n_items = rows * n_heads
    kern = functools.partial(_attn_kernel, rows=rows, n_heads=n_heads, n_blocks=n_blocks)
    stage = [pltpu.VMEM((n_items, MOBA_BLOCK, MOBA_BLOCK), F32),
             pltpu.VMEM((n_items, 1, MOBA_BLOCK), F32),
             pltpu.VMEM((n_items, 1, MOBA_BLOCK), F32)]
    last = n_blocks - 1
    return pl.pallas_call(
        kern,
        grid=(B // rows, n_blocks // 2),
        in_specs=[
            pl.BlockSpec((rows, A, MOBA_BLOCK), lambda b, p: (b, 0, p)),
            pl.BlockSpec((rows, A, MOBA_BLOCK), lambda b, p: (b, 0, last - p)),
            pl.BlockSpec((rows, S, A), lambda b, p: (b, 0, 0)),
            pl.BlockSpec((rows, n_blocks, n_heads * V_ROWS, MOBA_BLOCK), lambda b, p: (b, 0, 0, 0)),
            pl.BlockSpec((rows, n_blocks, A), lambda b, p: (b, 0, 0)),
        ],
        out_specs=pl.BlockSpec((rows, 2, 1, MOBA_BLOCK, A), lambda b, p: (b, 0, p, 0, 0)),
        out_shape=jax.ShapeDtypeStruct((B, 2, n_blocks // 2, MOBA_BLOCK, A), BF16),
        scratch_shapes=[
            pltpu.VMEM((2 * n_items, LANES, MOBA_BLOCK), BF16),
            pltpu.VMEM((rows * A // LANES, 2 * n_blocks, LANES), BF16),
            pltpu.VMEM((2, rows, n_heads * n_blocks, MOBA_BLOCK), F32),
            pltpu.VMEM((2 * n_items, 1, MOBA_BLOCK), F32),
            pltpu.VMEM((2 * n_items, V_ROWS, MOBA_BLOCK), F32),
        ] + stage + stage,
        compiler_params=pltpu.CompilerParams(dimension_semantics=("parallel", "arbitrary"),
                                             vmem_limit_bytes=VMEM_LIMIT),
        name="moba_attn",
    )(qT, qT, k, vT, km)


def _out_mlp_kernel(x_ref, attn_ref, pool_ref, mod_ref, g_post_ref, g_pre2_ref, g_post2_ref,
                    wo_ref, wup_ref, wdn_ref, o_ref, *, sub_rows, ff_chunk):
    A = attn_ref.shape[4]
    n_groups = x_ref.shape[1] // sub_rows
    half = pl.program_id(1)
    groups = [slice(r * sub_rows, (r + 1) * sub_rows) for r in range(n_groups)]
    ys = [jnp.dot(attn_ref[0, 0, jnp.where(half == 0, r, n_groups - 1 - r)], wo_ref[:A, :],
                  preferred_element_type=F32)
          + jnp.dot(pool_ref[0, rows, :], wo_ref[A:, :], preferred_element_type=F32)
          for r, rows in enumerate(groups)]
    gain1 = mod_ref[0, 2:3, :] * g_post_ref[...]
    gain2 = g_pre2_ref[...] * (1.0 + mod_ref[0, 4:5, :])
    gain3 = mod_ref[0, 5:6, :] * g_post2_ref[...]
    for rows, y in zip(groups, ys):
        x1 = x_ref[0, rows, :] + _rms(y) * gain1
        hb = (_rms(x1) * gain2 + mod_ref[0, 3:4, :]).astype(BF16)
        y2 = jnp.zeros_like(x1)
        for c in range(wup_ref.shape[1] // ff_chunk):
            cols = slice(c * ff_chunk, (c + 1) * ff_chunk)
            up = jnp.dot(hb, wup_ref[:, cols], preferred_element_type=F32)
            act = jnp.square(jnp.maximum(up, 0.0)).astype(BF16)
            y2 = y2 + jnp.dot(act, wdn_ref[cols, :], preferred_element_type=F32)
        o_ref[0, rows, :] = x1 + _rms(y2) * gain3


def _out_mlp(x, attn, pool, mod3, g_post, g_pre2, g_post2, wo, wup, wdn, *, tm, sub_rows,
             ff_chunk):
    B, S, D = x.shape
    A, P, FF = attn.shape[4], pool.shape[2], wup.shape[1]
    assert attn.shape[1:4] == (S // tm, tm // sub_rows, sub_rows)
    const = lambda *shape: pl.BlockSpec(shape, lambda b, i: (0,) * len(shape),
                                        pipeline_mode=pl.Buffered(1))
    return pl.pallas_call(
        functools.partial(_out_mlp_kernel, sub_rows=sub_rows, ff_chunk=ff_chunk),
        grid=(B, S // tm),
        in_specs=[
            pl.BlockSpec((1, tm, D), lambda b, i: (b, i, 0)),
            pl.BlockSpec((1, 1) + attn.shape[2:], lambda b, i: (b, i, 0, 0, 0)),
            pl.BlockSpec((1, tm, P), lambda b, i: (b, i, 0)),
            pl.BlockSpec((1, 6, D), lambda b, i: (b, 0, 0)),
            const(1, D), const(1, D), const(1, D),
            const(A + P, D), const(D, FF), const(FF, D),
        ],
        out_specs=pl.BlockSpec((1, tm, D), lambda b, i: (b, i, 0)),
        out_shape=jax.ShapeDtypeStruct((B, S, D), F32),
        compiler_params=pltpu.CompilerParams(dimension_semantics=("parallel", "parallel"),
                                             vmem_limit_bytes=VMEM_LIMIT),
        name="out_mlp",
    )(x, attn, pool, mod3, g_post, g_pre2, g_post2, wo, wup, wdn)


def _rope_tables(S):
    inv_freq = 1.0 / (ROPE_THETA ** (jnp.arange(HALF, dtype=F32) * (2.0 / HEAD_DIM)))
    ang = jnp.arange(S, dtype=jnp.int32).astype(F32)[:, None] * inv_freq[None, :]
    cos, sin = jnp.cos(ang), jnp.sin(ang)
    zero = jnp.zeros_like(sin)
    reps = LANES // HEAD_DIM
    cosn = jnp.tile(jnp.concatenate([cos, cos], axis=1), (1, reps))
    sina = jnp.tile(jnp.concatenate([-sin, zero], axis=1), (1, reps))
    sinb = jnp.tile(jnp.concatenate([zero, sin], axis=1), (1, reps))
    return cosn, sina, sinb, cos.T, sin.T


def kernel(x, c, w_ada, b_ada, g_mix_pre, g_mix_post, w_in, w_pool, pool_scale, w_out,
           g_mlp_pre, g_mlp_post, w_up, w_down):
    B, S, D = x.shape
    depth = w_ada.shape[0]
    P = pool_scale.shape[1]
    A = w_out.shape[1] - P
    tables = _rope_tables(S)
    for l in range(depth):
        mod3 = _adaln_mod(c, w_ada[l], b_ada[l]).reshape(B, 6, D)
        w_in_b = w_in[l].astype(BF16)
        wqT = w_in_b[:, :A].T
        wk = w_in_b[:, A:2 * A]
        wvT = w_in_b[:, 2 * A:3 * A].T
        wu = w_in_b[:, 3 * A:]
        qT, k, vT, pool_in, km = _mix_in(x, mod3, g_mix_pre[l][None, :], wqT, wk, wvT, wu, tables,
                                         tm=1024, sub_rows=256)
        wo = jnp.concatenate([w_out[l][:A], _fold_pool(w_pool[l], pool_scale[l][None, :],
                                                      w_out[l], A)], axis=0).astype(BF16)
        attn = _moba_attn(qT, k, vT, km, rows=2 if B % 2 == 0 else 1)
        x = _out_mlp(x, attn, pool_in, mod3, g_mix_post[l][None, :], g_mlp_pre[l][None, :],
                     g_mlp_post[l][None, :], wo, w_up[l].astype(BF16),
                     w_down[l].astype(BF16), tm=S // 2, sub_rows=MOBA_BLOCK, ff_chunk=1024)
    return x
```

```python
import functools

import jax
import jax.numpy as jnp
import numpy as np
from jax import lax
from jax.experimental import pallas as pl
from jax.experimental.pallas import tpu as pltpu

F32 = jnp.float32
BF16 = jnp.bfloat16

HEAD_DIM = 64
HALF = HEAD_DIM // 2
MOBA_BLOCK = 256
MOBA_TOPK = 3
POOL_WINDOWS = (2, 4, 8, 16)
POOL_GROUP = 128
MAX_WINDOW = max(POOL_WINDOWS)
POOL_CHUNK = 128
ROPE_THETA = 10000.0
NORM_EPS = 1e-6
Q_SCALE = HEAD_DIM ** -0.5 * 1.4426950408889634
BF16_ROWS = 16
V_ROWS = HEAD_DIM + BF16_ROWS
LANES = 128
PAIR = LANES // HEAD_DIM
VMEM_LIMIT = 56 * 1024 * 1024

NT_DIMS = (((1,), (1,)), ((), ()))


def _rms(x):
    return x * lax.rsqrt(jnp.mean(x * x, axis=-1, keepdims=True) + NORM_EPS)


def _mod_kernel(c_ref, w_ref, b_ref, o_ref):
    c = c_ref[...]
    act = c * jax.nn.sigmoid(c)
    n = act.shape[0]
    a_hi = act.astype(BF16)
    a_lo = (act - a_hi.astype(F32)).astype(BF16)
    w = w_ref[...]
    w_hi = w.astype(BF16)
    w_lo = (w - w_hi.astype(F32)).astype(BF16)
    both = jnp.dot(jnp.concatenate([a_hi, a_lo], axis=0), w_hi, preferred_element_type=F32)
    o_ref[...] = (both[:n] + both[n:] + jnp.dot(a_hi, w_lo, preferred_element_type=F32)
                  + b_ref[...])


def _adaln_mod(c, w_ada, b_ada):
    B, D = c.shape
    n_out = w_ada.shape[1]
    return pl.pallas_call(
        _mod_kernel,
        grid=(n_out // D,),
        in_specs=[pl.BlockSpec((B, D), lambda n: (0, 0)),
                  pl.BlockSpec((D, D), lambda n: (0, n)),
                  pl.BlockSpec((1, D), lambda n: (0, n))],
        out_specs=pl.BlockSpec((B, D), lambda n: (0, n)),
        out_shape=jax.ShapeDtypeStruct((B, n_out), F32),
        name="adaln_mod",
    )(c, w_ada, b_ada.reshape(1, n_out))


def _fold_pool_kernel(wpool_ref, pscale_ref, wo_ref, o_ref):
    for g in range(wpool_ref.shape[0]):
        cols = slice(g * POOL_GROUP, (g + 1) * POOL_GROUP)
        o_ref[cols, :] = jnp.dot(wpool_ref[g] * pscale_ref[:, cols], wo_ref[cols, :],
                                 precision=lax.Precision.HIGHEST, preferred_element_type=F32)


def _fold_pool(w_pool, pool_scale, w_out, attn_width):
    n_groups, P, D = w_pool.shape[0], pool_scale.shape[1], w_out.shape[1]
    assert attn_width % P == 0 and n_groups * POOL_GROUP == P
    return pl.pallas_call(
        _fold_pool_kernel,
        grid=(1,),
        in_specs=[pl.BlockSpec((n_groups, POOL_GROUP, POOL_GROUP), lambda n: (0, 0, 0)),
                  pl.BlockSpec((1, P), lambda n: (0, 0)),
                  pl.BlockSpec((P, D), lambda n: (attn_width // P, 0))],
        out_specs=pl.BlockSpec((P, D), lambda n: (0, 0)),
        out_shape=jax.ShapeDtypeStruct((P, D), F32),
        name="fold_pool",
    )(w_pool, pool_scale, w_out)


def _mix_in_kernel(x_ref, mod_ref, g_ref, wqT_ref, wk_ref, wvT_ref, wu_ref,
                   cosn_ref, sina_ref, sinb_ref, cost_ref, sint_ref,
                   qT_ref, k_ref, vT_ref, pool_ref, km_ref,
                   ext_ref, *, tm, sub_rows, n_heads):
    i = pl.program_id(1)
    blocks_per_group = sub_rows // MOBA_BLOCK

    @pl.when(i == 0)
    def _():
        ext_ref[:MAX_WINDOW, :] = jnp.zeros((MAX_WINDOW, ext_ref.shape[1]), F32)

    groups = [slice(r * sub_rows, (r + 1) * sub_rows) for r in range(tm // sub_rows)]
    gain = g_ref[...] * (1.0 + mod_ref[0, 1:2, :])
    hbs = []
    for rows in groups:
        x = x_ref[0, rows, :]
        hbs.append((_rms(x) * gain + mod_ref[0, 0:1, :]).astype(BF16))
    for gr, (rows, hb) in enumerate(zip(groups, hbs)):
        _mix_in_group(i * tm + gr * sub_rows, i * (tm // MOBA_BLOCK) + gr * blocks_per_group,
                      rows, gr * blocks_per_group, hb,
                      wqT_ref, wk_ref, wvT_ref, wu_ref,
                      cosn_ref, sina_ref, sinb_ref, cost_ref, sint_ref,
                      qT_ref, k_ref, vT_ref, pool_ref, km_ref, ext_ref, n_heads=n_heads)


def _mix_in_group(pos0, blk0, rows, tile_blk0, hb,
                  wqT_ref, wk_ref, wvT_ref, wu_ref,
                  cosn_ref, sina_ref, sinb_ref, cost_ref, sint_ref,
                  qT_ref, k_ref, vT_ref, pool_ref, km_ref, ext_ref, *, n_heads):
    tg = hb.shape[0]
    blocks_per_group = tg // MOBA_BLOCK
    k = jnp.dot(hb, wk_ref[...], preferred_element_type=F32)
    u = jnp.dot(hb, wu_ref[...], preferred_element_type=F32)
    qT = lax.dot_general(wqT_ref[...], hb, NT_DIMS, preferred_element_type=F32)
    vT = lax.dot_general(wvT_ref[...], hb, NT_DIMS, preferred_element_type=F32)

    cosn, sina, sinb = cosn_ref[rows, :], sina_ref[rows, :], sinb_ref[rows, :]
    k_chunks = []
    for c in range(k.shape[1] // LANES):
        kc = k[:, c * LANES:(c + 1) * LANES]
        kr = (kc * cosn + pltpu.roll(kc, LANES - HALF, axis=1) * sina
              + pltpu.roll(kc, HALF, axis=1) * sinb)
        k_chunks.append(kr)
        k_ref[0, rows, c * LANES:(c + 1) * LANES] = kr.astype(BF16)

    for t in range(blocks_per_group):
        blk = slice(t * MOBA_BLOCK, (t + 1) * MOBA_BLOCK)
        mean = jnp.concatenate([jnp.sum(kr[blk], axis=0, keepdims=True) for kr in k_chunks],
                               axis=1) * (1.0 / MOBA_BLOCK)
        km_ref[0, pl.ds(blk0 + t, 1), :] = mean

    cos_t, sin_t = cost_ref[:, rows], sint_ref[:, rows]
    q_parts = []
    for hd in range(n_heads):
        x1 = qT[hd * HEAD_DIM:hd * HEAD_DIM + HALF]
        x2 = qT[hd * HEAD_DIM + HALF:(hd + 1) * HEAD_DIM]
        q_parts.append(x1 * cos_t - x2 * sin_t)
        q_parts.append(x2 * cos_t + x1 * sin_t)
    qs = (jnp.concatenate(q_parts, axis=0) * Q_SCALE).astype(BF16)
    for t in range(blocks_per_group):
        qT_ref[0, tile_blk0 + t] = qs[:, t * MOBA_BLOCK:(t + 1) * MOBA_BLOCK]
    ones_rows = jnp.where(lax.broadcasted_iota(jnp.int32, (V_ROWS - HEAD_DIM, tg), 0) == 0, 1.0, 0.0)
    v_parts = []
    for hd in range(n_heads):
        v_parts += [vT[hd * HEAD_DIM:(hd + 1) * HEAD_DIM], ones_rows]
    v_aug = jnp.concatenate(v_parts, axis=0).astype(BF16)
    for t in range(blocks_per_group):
        vT_ref[0, tile_blk0 + t] = v_aug[:, t * MOBA_BLOCK:(t + 1) * MOBA_BLOCK]

    ext_ref[MAX_WINDOW:, :] = u
    for r0 in range(0, tg, POOL_CHUNK):
        pos1 = pos0 + r0 + lax.broadcasted_iota(jnp.int32, (POOL_CHUNK, 1), 0) + 1
        for gi, w in enumerate(POOL_WINDOWS):
            cols = slice(gi * POOL_GROUP, (gi + 1) * POOL_GROUP)
            win = ext_ref[r0:r0 + MAX_WINDOW + POOL_CHUNK, cols]
            token = win[MAX_WINDOW:, :]
            span = 1
            while span < w:
                win = win + pltpu.roll(win, span, axis=0)
                span *= 2
            count = jnp.minimum(pos1, w).astype(F32)
            pool_ref[0, rows.start + r0:rows.start + r0 + POOL_CHUNK, cols] = (
                win[MAX_WINDOW:, :] / count - token).astype(BF16)
    ext_ref[:MAX_WINDOW, :] = ext_ref[tg:, :]


def _mix_in(x, mod3, g_pre, wqT, wk, wvT, wu, tables, *, tm, sub_rows):
    B, S, D = x.shape
    A = wk.shape[1]
    P = wu.shape[1]
    n_heads = A // HEAD_DIM
    n_blocks = S // MOBA_BLOCK
    bpt = tm // MOBA_BLOCK
    cosn, sina, sinb, cost, sint = tables
    const = lambda *shape: pl.BlockSpec(shape, lambda b, i: (0,) * len(shape),
                                        pipeline_mode=pl.Buffered(1))
    kern = functools.partial(_mix_in_kernel, tm=tm, sub_rows=sub_rows, n_heads=n_heads)
    return pl.pallas_call(
        kern,
        grid=(B, S // tm),
        in_specs=[
            pl.BlockSpec((1, tm, D), lambda b, i: (b, i, 0)),
            pl.BlockSpec((1, 6, D), lambda b, i: (b, 0, 0)),
            const(1, D),
            const(A, D), const(D, A), const(A, D), const(D, P),
            pl.BlockSpec((tm, LANES), lambda b, i: (i, 0)),
            pl.BlockSpec((tm, LANES), lambda b, i: (i, 0)),
            pl.BlockSpec((tm, LANES), lambda b, i: (i, 0)),
            pl.BlockSpec((HALF, tm), lambda b, i: (0, i)),
            pl.BlockSpec((HALF, tm), lambda b, i: (0, i)),
        ],
        out_specs=[
            pl.BlockSpec((1, bpt, A, MOBA_BLOCK), lambda b, i: (b, i, 0, 0)),
            pl.BlockSpec((1, tm, A), lambda b, i: (b, i, 0)),
            pl.BlockSpec((1, bpt, n_heads * V_ROWS, MOBA_BLOCK), lambda b, i: (b, i, 0, 0)),
            pl.BlockSpec((1, tm, P), lambda b, i: (b, i, 0)),
            pl.BlockSpec((1, n_blocks, A), lambda b, i: (b, 0, 0)),
        ],
        out_shape=[
            jax.ShapeDtypeStruct((B, n_blocks, A, MOBA_BLOCK), BF16),
            jax.ShapeDtypeStruct((B, S, A), BF16),
            jax.ShapeDtypeStruct((B, n_blocks, n_heads * V_ROWS, MOBA_BLOCK), BF16),
            jax.ShapeDtypeStruct((B, S, P), BF16),
            jax.ShapeDtypeStruct((B, n_blocks, A), F32),
        ],
        scratch_shapes=[pltpu.VMEM((MAX_WINDOW + sub_rows, P), F32)],
        compiler_params=pltpu.CompilerParams(dimension_semantics=("parallel", "arbitrary"),
                                             vmem_limit_bytes=VMEM_LIMIT),
        name="mix_in",
    )(x, mod3, g_pre, wqT, wk, wvT, wu, cosn, sina, sinb, cost, sint)


def _attn_kernel(qa_ref, qb_ref, k_ref, vT_ref, km_ref, o_ref,
                 qz_ref, kmx_ref, sel_ref, m_ref, acc_ref,
                 s0_ref, alpha0_ref, shift0_ref, s1_ref, alpha1_ref, shift1_ref,
                 *, rows, n_heads, n_blocks):
    p = pl.program_id(1)
    q_blocks = (p, n_blocks - 1 - p)
    tq = MOBA_BLOCK
    items = [(r, hd) for r in range(rows) for hd in range(n_heads)]
    n_items = len(items)
    n_chunks = n_heads // PAIR
    halves = (list(enumerate(items))[:n_items // 2], list(enumerate(items))[n_items // 2:])
    zeros = jnp.zeros((HEAD_DIM, tq), BF16)
    for w, q_ref in enumerate((qa_ref, qb_ref)):
        for it, (r, hd) in enumerate(items):
            qh = q_ref[r, 0, hd * HEAD_DIM:(hd + 1) * HEAD_DIM, :]
            qz_ref[w * n_items + it] = jnp.concatenate(
                [qh, zeros] if hd % PAIR == 0 else [zeros, qh], axis=0)
    for r in range(rows):
        for c in range(n_chunks):
            km = km_ref[r, :, c * LANES:(c + 1) * LANES]
            km_hi = km.astype(BF16).astype(F32)
            kmx_ref[r * n_chunks + c] = jnp.concatenate([km_hi, km - km_hi], axis=0).astype(BF16)
    acc_ref[...] = jnp.zeros_like(acc_ref)
    buffers = ((s0_ref, alpha0_ref, shift0_ref), (s1_ref, alpha1_ref, shift1_ref))
    row_j = lax.broadcasted_iota(jnp.int32, (n_blocks, tq), 0)

    max_past = (n_blocks // 2 - 1, n_blocks - 1)

    def select_blocks(gate, n_valid):
        valid = row_j < n_valid
        if gate is None:
            return jnp.where(valid, 0.0, -jnp.inf)
        g = jnp.where(valid, gate, -jnp.inf)
        rank = jnp.zeros((n_blocks, tq), jnp.int32)
        for j in range(n_blocks):
            gj = jnp.broadcast_to(g[j:j + 1, :], g.shape)
            ahead = (gj > g) | ((gj == g) & (row_j > j))
            rank = rank + jnp.where(ahead, 1, 0)
        return jnp.where(valid & (rank < MOBA_TOPK), 0.0, -jnp.inf)

    def pass1(half, w, j, own, dst):
        s_ref, alpha_ref, shift_ref = dst
        keys = pl.ds(pl.multiple_of(j * tq, tq), tq)
        qz_w, m_w = qz_ref.at[pl.ds(w * n_items, n_items)], m_ref.at[pl.ds(w * n_items, n_items)]
        for it, (r, hd) in halves[half]:
            c = hd // PAIR
            kp = k_ref[r, keys, c * LANES:(c + 1) * LANES]
            gated = own and max_past[w] > MOBA_TOPK
            if gated:
                kp = jnp.concatenate([kp, kmx_ref[r * n_chunks + c]], axis=0)
            s = jnp.dot(kp, qz_w[it], preferred_element_type=F32)
            if own:
                gate = s[tq:tq + n_blocks] + s[tq + n_blocks:] if gated else None
                sel_ref[w, r, hd * n_blocks:(hd + 1) * n_blocks, :] = select_blocks(gate, j)
                key_i = lax.broadcasted_iota(jnp.int32, (tq, tq), 0)
                qry_i = lax.broadcasted_iota(jnp.int32, (tq, tq), 1)
                s = jnp.where(key_i <= qry_i, s[:tq], -jnp.inf)
            s_ref[it] = s
            cmax = jnp.max(s, axis=0, keepdims=True)
            if own:
                m_w[it] = cmax
                alpha_ref[it] = jnp.zeros_like(cmax)
                shift_ref[it] = cmax
            else:
                bias = sel_ref[w, r, pl.ds(hd * n_blocks + j, 1), :]
                m_old = m_w[it]
                m_new = jnp.maximum(m_old, cmax + bias)
                m_w[it] = m_new
                alpha_ref[it] = jnp.exp2(m_old - m_new)
                shift_ref[it] = m_new - bias

    def pass2(half, w, j, src):
        s_ref, alpha_ref, shift_ref = src
        acc_w = acc_ref.at[pl.ds(w * n_items, n_items)]
        for it, (r, hd) in halves[half]:
            prob = jnp.exp2((s_ref[it] - shift_ref[it]).astype(BF16))
            pv = jnp.dot(vT_ref[r, j, hd * V_ROWS:(hd + 1) * V_ROWS, :], prob,
                         preferred_element_type=F32)
            acc_w[it] = alpha_ref[it] * acc_w[it] + pv

    n_elems = n_blocks + 1

    def elem(n):
        if isinstance(n, int) and n < 2:
            return n, q_blocks[n], True
        e = n - 2
        first = e < q_blocks[0]
        return jnp.where(first, 0, 1), jnp.where(first, e, e - q_blocks[0]), False

    def turn(t, parity):
        for half, n in ((0, t), (1, t - 1)):
            static = isinstance(n, int)
            if not static or 1 <= n <= n_elems:
                w, j, _ = elem(n - 1)
                pass2(half, w, j, buffers[(parity + 1 + half) % 2])
            if not static or 0 <= n < n_elems:
                w, j, own = elem(n)
                pass1(half, w, j, own, buffers[(parity + half) % 2])

    first_rolled, n_rolled = 4, 2 * ((n_elems - 4) // 2)
    for t in range(first_rolled):
        turn(t, t % 2)

    def two_turns(u, carry):
        t = first_rolled + 2 * u
        turn(t, first_rolled % 2)
        turn(t + 1, (first_rolled + 1) % 2)
        return carry

    lax.fori_loop(0, n_rolled // 2, two_turns, 0)
    for t in range(first_rolled + n_rolled, n_elems + 2):
        turn(t, t % 2)

    for w in range(2):
        for r in range(rows):
            for c in range(n_heads // PAIR):
                pair = [acc_ref[it, :HEAD_DIM] * (1.0 / acc_ref[it, HEAD_DIM:HEAD_DIM + 1])
                        for it in range(w * n_items + r * n_heads + c * PAIR,
                                        w * n_items + r * n_heads + (c + 1) * PAIR)]
                o_ref[r, w, 0, :, c * LANES:(c + 1) * LANES] = (
                    jnp.concatenate(pair, axis=0).T.astype(BF16))


def _moba_attn(qT, k, vT, km, *, rows):
    B, n_blocks, A, _ = qT.shape
    S = n_blocks * MOBA_BLOCK
    n_heads = A // HEAD_DIM
    assert n_blocks % 2 == 0 and (n_blocks - 1) % 2 == 1
    n_items = rows * n_heads
    kern = functools.partial(_attn_kernel, rows=rows, n_heads=n_heads, n_blocks=n_blocks)
    stage = [pltpu.VMEM((n_items, MOBA_BLOCK, MOBA_BLOCK), F32),
             pltpu.VMEM((n_items, 1, MOBA_BLOCK), F32),
             pltpu.VMEM((n_items, 1, MOBA_BLOCK), F32)]
    last = n_blocks - 1
    return pl.pallas_call(
        kern,
        grid=(B // rows, n_blocks // 2),
        in_specs=[
            pl.BlockSpec((rows, 1, A, MOBA_BLOCK), lambda b, p: (b, p, 0, 0)),
            pl.BlockSpec((rows, 1, A, MOBA_BLOCK), lambda b, p: (b, last - p, 0, 0)),
            pl.BlockSpec((rows, S, A), lambda b, p: (b, 0, 0)),
            pl.BlockSpec((rows, n_blocks, n_heads * V_ROWS, MOBA_BLOCK), lambda b, p: (b, 0, 0, 0)),
            pl.BlockSpec((rows, n_blocks, A), lambda b, p: (b, 0, 0)),
        ],
        out_specs=pl.BlockSpec((rows, 2, 1, MOBA_BLOCK, A), lambda b, p: (b, 0, p, 0, 0)),
        out_shape=jax.ShapeDtypeStruct((B, 2, n_blocks // 2, MOBA_BLOCK, A), BF16),
        scratch_shapes=[
            pltpu.VMEM((2 * n_items, LANES, MOBA_BLOCK), BF16),
            pltpu.VMEM((rows * A // LANES, 2 * n_blocks, LANES), BF16),
            pltpu.VMEM((2, rows, n_heads * n_blocks, MOBA_BLOCK), F32),
            pltpu.VMEM((2 * n_items, 1, MOBA_BLOCK), F32),
            pltpu.VMEM((2 * n_items, V_ROWS, MOBA_BLOCK), F32),
        ] + stage + stage,
        compiler_params=pltpu.CompilerParams(dimension_semantics=("parallel", "arbitrary"),
                                             vmem_limit_bytes=VMEM_LIMIT),
        name="moba_attn",
    )(qT, qT, k, vT, km)


def _out_mlp_kernel(x_ref, attn_ref, pool_ref, mod_ref, g_post_ref, g_pre2_ref, g_post2_ref,
                    wo_ref, wup_ref, wdn_ref, o_ref, *, sub_rows, ff_chunk):
    A = attn_ref.shape[4]
    n_groups = x_ref.shape[1] // sub_rows
    half = pl.program_id(1)
    groups = [slice(r * sub_rows, (r + 1) * sub_rows) for r in range(n_groups)]
    ys = [jnp.dot(attn_ref[0, 0, jnp.where(half == 0, r, n_groups - 1 - r)], wo_ref[:A, :],
                  preferred_element_type=F32)
          + jnp.dot(pool_ref[0, rows, :], wo_ref[A:, :], preferred_element_type=F32)
          for r, rows in enumerate(groups)]
    gain1 = mod_ref[0, 2:3, :] * g_post_ref[...]
    gain2 = g_pre2_ref[...] * (1.0 + mod_ref[0, 4:5, :])
    gain3 = mod_ref[0, 5:6, :] * g_post2_ref[...]
    for rows, y in zip(groups, ys):
        x1 = x_ref[0, rows, :] + _rms(y) * gain1
        hb = (_rms(x1) * gain2 + mod_ref[0, 3:4, :]).astype(BF16)
        y2 = jnp.zeros_like(x1)
        for c in range(wup_ref.shape[1] // ff_chunk):
            cols = slice(c * ff_chunk, (c + 1) * ff_chunk)
            up = jnp.dot(hb, wup_ref[:, cols], preferred_element_type=F32)
            act = jnp.square(jnp.maximum(up, 0.0)).astype(BF16)
            y2 = y2 + jnp.dot(act, wdn_ref[cols, :], preferred_element_type=F32)
        o_ref[0, rows, :] = x1 + _rms(y2) * gain3


def _out_mlp(x, attn, pool, mod3, g_post, g_pre2, g_post2, wo, wup, wdn, *, tm, sub_rows,
             ff_chunk):
    B, S, D = x.shape
    A, P, FF = attn.shape[4], pool.shape[2], wup.shape[1]
    assert attn.shape[1:4] == (S // tm, tm // sub_rows, sub_rows)
    const = lambda *shape: pl.BlockSpec(shape, lambda b, i: (0,) * len(shape),
                                        pipeline_mode=pl.Buffered(1))
    return pl.pallas_call(
        functools.partial(_out_mlp_kernel, sub_rows=sub_rows, ff_chunk=ff_chunk),
        grid=(B, S // tm),
        in_specs=[
            pl.BlockSpec((1, tm, D), lambda b, i: (b, i, 0)),
            pl.BlockSpec((1, 1) + attn.shape[2:], lambda b, i: (b, i, 0, 0, 0)),
            pl.BlockSpec((1, tm, P), lambda b, i: (b, i, 0)),
            pl.BlockSpec((1, 6, D), lambda b, i: (b, 0, 0)),
            const(1, D), const(1, D), const(1, D),
            const(A + P, D), const(D, FF), const(FF, D),
        ],
        out_specs=pl.BlockSpec((1, tm, D), lambda b, i: (b, i, 0)),
        out_shape=jax.ShapeDtypeStruct((B, S, D), F32),
        compiler_params=pltpu.CompilerParams(dimension_semantics=("parallel", "parallel"),
                                             vmem_limit_bytes=VMEM_LIMIT),
        name="out_mlp",
    )(x, attn, pool, mod3, g_post, g_pre2, g_post2, wo, wup, wdn)


def _rope_tables(S):
    inv_freq = 1.0 / (ROPE_THETA ** (np.arange(HALF, dtype=np.float64) * (2.0 / HEAD_DIM)))
    ang = np.arange(S, dtype=np.float64)[:, None] * inv_freq[None, :]
    cos, sin = np.cos(ang).astype(np.float32), np.sin(ang).astype(np.float32)
    zero = np.zeros_like(sin)
    reps = LANES // HEAD_DIM
    cosn = np.tile(np.concatenate([cos, cos], axis=1), (1, reps))
    sina = np.tile(np.concatenate([-sin, zero], axis=1), (1, reps))
    sinb = np.tile(np.concatenate([zero, sin], axis=1), (1, reps))
    return tuple(jnp.asarray(t) for t in (cosn, sina, sinb, np.ascontiguousarray(cos.T),
                                          np.ascontiguousarray(sin.T)))


def kernel(x, c, w_ada, b_ada, g_mix_pre, g_mix_post, w_in, w_pool, pool_scale, w_out,
           g_mlp_pre, g_mlp_post, w_up, w_down):
    B, S, D = x.shape
    depth = w_ada.shape[0]
    P = pool_scale.shape[1]
    A = w_out.shape[1] - P
    tables = _rope_tables(S)
    for l in range(depth):
        mod3 = _adaln_mod(c, w_ada[l], b_ada[l]).reshape(B, 6, D)
        w_in_b = w_in[l].astype(BF16)
        wqT = w_in_b[:, :A].T
        wk = w_in_b[:, A:2 * A]
        wvT = w_in_b[:, 2 * A:3 * A].T
        wu = w_in_b[:, 3 * A:]
        qT, k, vT, pool_in, km = _mix_in(x, mod3, g_mix_pre[l][None, :], wqT, wk, wvT, wu, tables,
                                         tm=1024, sub_rows=256)
        wo = jnp.concatenate([w_out[l][:A], _fold_pool(w_pool[l], pool_scale[l][None, :],
                                                      w_out[l], A)], axis=0).astype(BF16)
        attn = _moba_attn(qT, k, vT, km, rows=2 if B % 2 == 0 else 1)
        x = _out_mlp(x, attn, pool_in, mod3, g_mix_post[l][None, :], g_mlp_pre[l][None, :],
                     g_mlp_post[l][None, :], wo, w_up[l].astype(BF16),
                     w_down[l].astype(BF16), tm=S // 2, sub_rows=MOBA_BLOCK, ff_chunk=1024)
    return x
```

```python
import functools

import jax
import jax.numpy as jnp
import numpy as np
from jax import lax
from jax.experimental import pallas as pl
from jax.experimental.pallas import tpu as pltpu

F32 = jnp.float32
BF16 = jnp.bfloat16

HEAD_DIM = 64
HALF = HEAD_DIM // 2
MOBA_BLOCK = 256
MOBA_TOPK = 3
POOL_WINDOWS = (2, 4, 8, 16)
POOL_GROUP = 128
MAX_WINDOW = max(POOL_WINDOWS)
POOL_CHUNK = 128
ROPE_THETA = 10000.0
NORM_EPS = 1e-6
Q_SCALE = HEAD_DIM ** -0.5 * 1.4426950408889634
BF16_ROWS = 16
V_ROWS = HEAD_DIM + BF16_ROWS
LANES = 128
PAIR = LANES // HEAD_DIM
VMEM_LIMIT = 56 * 1024 * 1024

NT_DIMS = (((1,), (1,)), ((), ()))


def _rms(x):
    return x * lax.rsqrt(jnp.mean(x * x, axis=-1, keepdims=True) + NORM_EPS)


def _mod_kernel(c_ref, w_ref, b_ref, o_ref):
    c = c_ref[...]
    act = c * jax.nn.sigmoid(c)
    n = act.shape[0]
    a_hi = act.astype(BF16)
    a_lo = (act - a_hi.astype(F32)).astype(BF16)
    w = w_ref[...]
    w_hi = w.astype(BF16)
    w_lo = (w - w_hi.astype(F32)).astype(BF16)
    both = jnp.dot(jnp.concatenate([a_hi, a_lo], axis=0), w_hi, preferred_element_type=F32)
    o_ref[...] = (both[:n] + both[n:] + jnp.dot(a_hi, w_lo, preferred_element_type=F32)
                  + b_ref[...])


def _adaln_mod(c, w_ada, b_ada):
    B, D = c.shape
    n_out = w_ada.shape[1]
    return pl.pallas_call(
        _mod_kernel,
        grid=(n_out // D,),
        in_specs=[pl.BlockSpec((B, D), lambda n: (0, 0)),
                  pl.BlockSpec((D, D), lambda n: (0, n)),
                  pl.BlockSpec((1, D), lambda n: (0, n))],
        out_specs=pl.BlockSpec((B, D), lambda n: (0, n)),
        out_shape=jax.ShapeDtypeStruct((B, n_out), F32),
        name="adaln_mod",
    )(c, w_ada, b_ada.reshape(1, n_out))


def _fold_pool_kernel(wpool_ref, pscale_ref, wo_ref, o_ref):
    for g in range(wpool_ref.shape[0]):
        cols = slice(g * POOL_GROUP, (g + 1) * POOL_GROUP)
        o_ref[cols, :] = jnp.dot(wpool_ref[g] * pscale_ref[:, cols], wo_ref[cols, :],
                                 precision=lax.Precision.HIGHEST, preferred_element_type=F32)


def _fold_pool(w_pool, pool_scale, w_out, attn_width):
    n_groups, P, D = w_pool.shape[0], pool_scale.shape[1], w_out.shape[1]
    assert attn_width % P == 0 and n_groups * POOL_GROUP == P
    return pl.pallas_call(
        _fold_pool_kernel,
        grid=(1,),
        in_specs=[pl.BlockSpec((n_groups, POOL_GROUP, POOL_GROUP), lambda n: (0, 0, 0)),
                  pl.BlockSpec((1, P), lambda n: (0, 0)),
                  pl.BlockSpec((P, D), lambda n: (attn_width // P, 0))],
        out_specs=pl.BlockSpec((P, D), lambda n: (0, 0)),
        out_shape=jax.ShapeDtypeStruct((P, D), F32),
        name="fold_pool",
    )(w_pool, pool_scale, w_out)


def _mix_in_kernel(x_ref, mod_ref, g_ref, wqT_ref, wk_ref, wvT_ref, wu_ref,
                   cosn_ref, sina_ref, sinb_ref, cost_ref, sint_ref,
                   qT_ref, k_ref, vT_ref, pool_ref, km_ref,
                   ext_ref, *, tm, sub_rows, n_heads):
    i = pl.program_id(1)
    blocks_per_group = sub_rows // MOBA_BLOCK

    @pl.when(i == 0)
    def _():
        ext_ref[:MAX_WINDOW, :] = jnp.zeros((MAX_WINDOW, ext_ref.shape[1]), F32)

    groups = [slice(r * sub_rows, (r + 1) * sub_rows) for r in range(tm // sub_rows)]
    gain = g_ref[...] * (1.0 + mod_ref[0, 1:2, :])
    hbs = []
    for rows in groups:
        x = x_ref[0, rows, :]
        hbs.append((_rms(x) * gain + mod_ref[0, 0:1, :]).astype(BF16))
    for gr, (rows, hb) in enumerate(zip(groups, hbs)):
        _mix_in_group(i * tm + gr * sub_rows, i * (tm // MOBA_BLOCK) + gr * blocks_per_group,
                      rows, gr * blocks_per_group, hb,
                      wqT_ref, wk_ref, wvT_ref, wu_ref,
                      cosn_ref, sina_ref, sinb_ref, cost_ref, sint_ref,
                      qT_ref, k_ref, vT_ref, pool_ref, km_ref, ext_ref, n_heads=n_heads)


def _mix_in_group(pos0, blk0, rows, tile_blk0, hb,
                  wqT_ref, wk_ref, wvT_ref, wu_ref,
                  cosn_ref, sina_ref, sinb_ref, cost_ref, sint_ref,
                  qT_ref, k_ref, vT_ref, pool_ref, km_ref, ext_ref, *, n_heads):
    tg = hb.shape[0]
    blocks_per_group = tg // MOBA_BLOCK
    k = jnp.dot(hb, wk_ref[...], preferred_element_type=F32)
    u = jnp.dot(hb, wu_ref[...], preferred_element_type=F32)
    qT = lax.dot_general(wqT_ref[...], hb, NT_DIMS, preferred_element_type=F32)
    vT = lax.dot_general(wvT_ref[...], hb, NT_DIMS, preferred_element_type=F32)

    cosn, sina, sinb = cosn_ref[rows, :], sina_ref[rows, :], sinb_ref[rows, :]
    k_chunks = []
    for c in range(k.shape[1] // LANES):
        kc = k[:, c * LANES:(c + 1) * LANES]
        kr = (kc * cosn + pltpu.roll(kc, LANES - HALF, axis=1) * sina
              + pltpu.roll(kc, HALF, axis=1) * sinb)
        k_chunks.append(kr)
        k_ref[0, rows, c * LANES:(c + 1) * LANES] = kr.astype(BF16)

    for t in range(blocks_per_group):
        blk = slice(t * MOBA_BLOCK, (t + 1) * MOBA_BLOCK)
        mean = jnp.concatenate([jnp.sum(kr[blk], axis=0, keepdims=True) for kr in k_chunks],
                               axis=1) * (1.0 / MOBA_BLOCK)
        km_ref[0, pl.ds(blk0 + t, 1), :] = mean

    cos_t, sin_t = cost_ref[:, rows], sint_ref[:, rows]
    q_parts = []
    for hd in range(n_heads):
        x1 = qT[hd * HEAD_DIM:hd * HEAD_DIM + HALF]
        x2 = qT[hd * HEAD_DIM + HALF:(hd + 1) * HEAD_DIM]
        q_parts.append(x1 * cos_t - x2 * sin_t)
        q_parts.append(x2 * cos_t + x1 * sin_t)
    qs = (jnp.concatenate(q_parts, axis=0) * Q_SCALE).astype(BF16)
    for t in range(blocks_per_group):
        qT_ref[0, tile_blk0 + t] = qs[:, t * MOBA_BLOCK:(t + 1) * MOBA_BLOCK]
    ones_rows = jnp.where(lax.broadcasted_iota(jnp.int32, (V_ROWS - HEAD_DIM, tg), 0) == 0, 1.0, 0.0)
    v_parts = []
    for hd in range(n_heads):
        v_parts += [vT[hd * HEAD_DIM:(hd + 1) * HEAD_DIM], ones_rows]
    v_aug = jnp.concatenate(v_parts, axis=0).astype(BF16)
    for t in range(blocks_per_group):
        vT_ref[0, tile_blk0 + t] = v_aug[:, t * MOBA_BLOCK:(t + 1) * MOBA_BLOCK]

    ext_ref[MAX_WINDOW:, :] = u
    for r0 in range(0, tg, POOL_CHUNK):
        pos1 = pos0 + r0 + lax.broadcasted_iota(jnp.int32, (POOL_CHUNK, 1), 0) + 1
        for gi, w in enumerate(POOL_WINDOWS):
            cols = slice(gi * POOL_GROUP, (gi + 1) * POOL_GROUP)
            win = ext_ref[r0:r0 + MAX_WINDOW + POOL_CHUNK, cols]
            token = win[MAX_WINDOW:, :]
            span = 1
            while span < w:
                win = win + pltpu.roll(win, span, axis=0)
                span *= 2
            count = jnp.minimum(pos1, w).astype(F32)
            pool_ref[0, rows.start + r0:rows.start + r0 + POOL_CHUNK, cols] = (
                win[MAX_WINDOW:, :] / count - token).astype(BF16)
    ext_ref[:MAX_WINDOW, :] = ext_ref[tg:, :]


def _mix_in(x, mod3, g_pre, wqT, wk, wvT, wu, tables, *, tm, sub_rows):
    B, S, D = x.shape
    A = wk.shape[1]
    P = wu.shape[1]
    n_heads = A // HEAD_DIM
    n_blocks = S // MOBA_BLOCK
    bpt = tm // MOBA_BLOCK
    cosn, sina, sinb, cost, sint = tables
    const = lambda *shape: pl.BlockSpec(shape, lambda b, i: (0,) * len(shape),
                                        pipeline_mode=pl.Buffered(1))
    kern = functools.partial(_mix_in_kernel, tm=tm, sub_rows=sub_rows, n_heads=n_heads)
    return pl.pallas_call(
        kern,
        grid=(B, S // tm),
        in_specs=[
            pl.BlockSpec((1, tm, D), lambda b, i: (b, i, 0)),
            pl.BlockSpec((1, 6, D), lambda b, i: (b, 0, 0)),
            const(1, D),
            const(A, D), const(D, A), const(A, D), const(D, P),
            pl.BlockSpec((tm, LANES), lambda b, i: (i, 0)),
            pl.BlockSpec((tm, LANES), lambda b, i: (i, 0)),
            pl.BlockSpec((tm, LANES), lambda b, i: (i, 0)),
            pl.BlockSpec((HALF, tm), lambda b, i: (0, i)),
            pl.BlockSpec((HALF, tm), lambda b, i: (0, i)),
        ],
        out_specs=[
            pl.BlockSpec((1, bpt, A, MOBA_BLOCK), lambda b, i: (b, i, 0, 0)),
            pl.BlockSpec((1, tm, A), lambda b, i: (b, i, 0)),
            pl.BlockSpec((1, bpt, n_heads * V_ROWS, MOBA_BLOCK), lambda b, i: (b, i, 0, 0)),
            pl.BlockSpec((1, tm, P), lambda b, i: (b, i, 0)),
            pl.BlockSpec((1, n_blocks, A), lambda b, i: (b, 0, 0)),
        ],
        out_shape=[
            jax.ShapeDtypeStruct((B, n_blocks, A, MOBA_BLOCK), BF16),
            jax.ShapeDtypeStruct((B, S, A), BF16),
            jax.ShapeDtypeStruct((B, n_blocks, n_heads * V_ROWS, MOBA_BLOCK), BF16),
            jax.ShapeDtypeStruct((B, S, P), BF16),
            jax.ShapeDtypeStruct((B, n_blocks, A), F32),
        ],
        scratch_shapes=[pltpu.VMEM((MAX_WINDOW + sub_rows, P), F32)],
        compiler_params=pltpu.CompilerParams(dimension_semantics=("parallel", "arbitrary"),
                                             vmem_limit_bytes=VMEM_LIMIT),
        name="mix_in",
    )(x, mod3, g_pre, wqT, wk, wvT, wu, cosn, sina, sinb, cost, sint)


def _attn_kernel(qa_ref, qb_ref, k_ref, vT_ref, km_ref, o_ref,
                 qz_ref, kmx_ref, sel_ref, m_ref, acc_ref,
                 s0_ref, alpha0_ref, shift0_ref, s1_ref, alpha1_ref, shift1_ref,
                 *, rows, n_heads, n_blocks):
    p = pl.program_id(1)
    q_blocks = (p, n_blocks - 1 - p)
    tq = MOBA_BLOCK
    items = [(r, hd) for r in range(rows) for hd in range(n_heads)]
    n_items = len(items)
    n_chunks = n_heads // PAIR
    halves = (list(enumerate(items))[:n_items // 2], list(enumerate(items))[n_items // 2:])
    zeros = jnp.zeros((HEAD_DIM, tq), BF16)
    for w, q_ref in enumerate((qa_ref, qb_ref)):
        for it, (r, hd) in enumerate(items):
            qh = q_ref[r, 0, hd * HEAD_DIM:(hd + 1) * HEAD_DIM, :]
            qz_ref[w * n_items + it] = jnp.concatenate(
                [qh, zeros] if hd % PAIR == 0 else [zeros, qh], axis=0)
    for r in range(rows):
        for c in range(n_chunks):
            km = km_ref[r, :, c * LANES:(c + 1) * LANES]
            km_hi = km.astype(BF16).astype(F32)
            kmx_ref[r * n_chunks + c] = jnp.concatenate([km_hi, km - km_hi], axis=0).astype(BF16)
    acc_ref[...] = jnp.zeros_like(acc_ref)
    buffers = ((s0_ref, alpha0_ref, shift0_ref), (s1_ref, alpha1_ref, shift1_ref))
    row_j = lax.broadcasted_iota(jnp.int32, (n_blocks, tq), 0)

    max_past = (n_blocks // 2 - 1, n_blocks - 1)

    def select_blocks(gate, n_valid):
        valid = row_j < n_valid
        if gate is None:
            return jnp.where(valid, 0.0, -jnp.inf)
        g = jnp.where(valid, gate, -jnp.inf)
        rank = jnp.zeros((n_blocks, tq), jnp.int32)
        for j in range(n_blocks):
            gj = jnp.broadcast_to(g[j:j + 1, :], g.shape)
            ahead = (gj > g) | ((gj == g) & (row_j > j))
            rank = rank + jnp.where(ahead, 1, 0)
        return jnp.where(valid & (rank < MOBA_TOPK), 0.0, -jnp.inf)

    def pass1(half, w, j, own, dst):
        s_ref, alpha_ref, shift_ref = dst
        keys = pl.ds(pl.multiple_of(j * tq, tq), tq)
        qz_w, m_w = qz_ref.at[pl.ds(w * n_items, n_items)], m_ref.at[pl.ds(w * n_items, n_items)]
        for it, (r, hd) in halves[half]:
            c = hd // PAIR
            kp = k_ref[r, keys, c * LANES:(c + 1) * LANES]
            gated = own and max_past[w] > MOBA_TOPK
            if gated:
                kp = jnp.concatenate([kp, kmx_ref[r * n_chunks + c]], axis=0)
            s = jnp.dot(kp, qz_w[it], preferred_element_type=F32)
            if own:
                gate = s[tq:tq + n_blocks] + s[tq + n_blocks:] if gated else None
                sel_ref[w, r, hd * n_blocks:(hd + 1) * n_blocks, :] = select_blocks(gate, j)
                key_i = lax.broadcasted_iota(jnp.int32, (tq, tq), 0)
                qry_i = lax.broadcasted_iota(jnp.int32, (tq, tq), 1)
                s = jnp.where(key_i <= qry_i, s[:tq], -jnp.inf)
            s_ref[it] = s
            cmax = jnp.max(s, axis=0, keepdims=True)
            if own:
                m_w[it] = cmax
                alpha_ref[it] = jnp.zeros_like(cmax)
                shift_ref[it] = cmax
            else:
                bias = sel_ref[w, r, pl.ds(hd * n_blocks + j, 1), :]
                m_old = m_w[it]
                m_new = jnp.maximum(m_old, cmax + bias)
                m_w[it] = m_new
                alpha_ref[it] = jnp.exp2(m_old - m_new)
                shift_ref[it] = m_new - bias

    def pass2(half, w, j, src):
        s_ref, alpha_ref, shift_ref = src
        acc_w = acc_ref.at[pl.ds(w * n_items, n_items)]
        for it, (r, hd) in halves[half]:
            prob = jnp.exp2(s_ref[it] - shift_ref[it]).astype(BF16)
            pv = jnp.dot(vT_ref[r, j, hd * V_ROWS:(hd + 1) * V_ROWS, :], prob,
                         preferred_element_type=F32)
            acc_w[it] = alpha_ref[it] * acc_w[it] + pv

    n_elems = n_blocks + 1

    def elem(n):
        if isinstance(n, int) and n < 2:
            return n, q_blocks[n], True
        e = n - 2
        first = e < q_blocks[0]
        return jnp.where(first, 0, 1), jnp.where(first, e, e - q_blocks[0]), False

    def turn(t, parity):
        for half, n in ((0, t), (1, t - 1)):
            static = isinstance(n, int)
            if not static or 1 <= n <= n_elems:
                w, j, _ = elem(n - 1)
                pass2(half, w, j, buffers[(parity + 1 + half) % 2])
            if not static or 0 <= n < n_elems:
                w, j, own = elem(n)
                pass1(half, w, j, own, buffers[(parity + half) % 2])

    first_rolled, n_rolled = 4, 2 * ((n_elems - 4) // 2)
    for t in range(first_rolled):
        turn(t, t % 2)

    def two_turns(u, carry):
        t = first_rolled + 2 * u
        turn(t, first_rolled % 2)
        turn(t + 1, (first_rolled + 1) % 2)
        return carry

    lax.fori_loop(0, n_rolled // 2, two_turns, 0)
    for t in range(first_rolled + n_rolled, n_elems + 2):
        turn(t, t % 2)

    for w in range(2):
        for r in range(rows):
            for c in range(n_heads // PAIR):
                pair = [acc_ref[it, :HEAD_DIM] * (1.0 / acc_ref[it, HEAD_DIM:HEAD_DIM + 1])
                        for it in range(w * n_items + r * n_heads + c * PAIR,
                                        w * n_items + r * n_heads + (c + 1) * PAIR)]
                o_ref[r, w, 0, :, c * LANES:(c + 1) * LANES] = (
                    jnp.concatenate(pair, axis=0).T.astype(BF16))


def _moba_attn(qT, k, vT, km, *, rows):
    B, n_blocks, A, _ = qT.shape
    S = n_blocks * MOBA_BLOCK
    n_heads = A // HEAD_DIM
    assert n_blocks % 2 == 0 and (n_blocks - 1) % 2 == 1
    n_items = rows * n_heads
    kern = functools.partial(_attn_kernel, rows=rows, n_heads=n_heads, n_blocks=n_blocks)
    stage = [pltpu.VMEM((n_items, MOBA_BLOCK, MOBA_BLOCK), F32),
             pltpu.VMEM((n_items, 1, MOBA_BLOCK), F32),
             pltpu.VMEM((n_items, 1, MOBA_BLOCK), F32)]
    last = n_blocks - 1
    return pl.pallas_call(
        kern,
        grid=(B // rows, n_blocks // 2),
        in_specs=[
            pl.BlockSpec((rows, 1, A, MOBA_BLOCK), lambda b, p: (b, p, 0, 0)),
            pl.BlockSpec((rows, 1, A, MOBA_BLOCK), lambda b, p: (b, last - p, 0, 0)),
            pl.BlockSpec((rows, S, A), lambda b, p: (b, 0, 0)),
            pl.BlockSpec((rows, n_blocks, n_heads * V_ROWS, MOBA_BLOCK), lambda b, p: (b, 0, 0, 0)),
            pl.BlockSpec((rows, n_blocks, A), lambda b, p: (b, 0, 0)),
        ],
        out_specs=pl.BlockSpec((rows, 2, 1, MOBA_BLOCK, A), lambda b, p: (b, 0, p, 0, 0)),
        out_shape=jax.ShapeDtypeStruct((B, 2, n_blocks // 2, MOBA_BLOCK, A), BF16),
        scratch_shapes=[
            pltpu.VMEM((2 * n_items, LANES, MOBA_BLOCK), BF16),
            pltpu.VMEM((rows * A // LANES, 2 * n_blocks, LANES), BF16),
            pltpu.VMEM((2, rows, n_heads * n_blocks, MOBA_BLOCK), F32),
            pltpu.VMEM((2 * n_items, 1, MOBA_BLOCK), F32),
            pltpu.VMEM((2 * n_items, V_ROWS, MOBA_BLOCK), F32),
        ] + stage + stage,
        compiler_params=pltpu.CompilerParams(dimension_semantics=("parallel", "arbitrary"),
                                             vmem_limit_bytes=VMEM_LIMIT),
        name="moba_attn",
    )(qT, qT, k, vT, km)


def _out_mlp_kernel(x_ref, attn_ref, pool_ref, mod_ref, g_post_ref, g_pre2_ref, g_post2_ref,
                    wo_ref, wup_ref, wdn_ref, o_ref, *, sub_rows, ff_chunk):
    A = attn_ref.shape[4]
    n_groups = x_ref.shape[1] // sub_rows
    half = pl.program_id(1)
    groups = [slice(r * sub_rows, (r + 1) * sub_rows) for r in range(n_groups)]
    ys = [jnp.dot(attn_ref[0, 0, jnp.where(half == 0, r, n_groups - 1 - r)], wo_ref[:A, :],
                  preferred_element_type=F32)
          + jnp.dot(pool_ref[0, rows, :], wo_ref[A:, :], preferred_element_type=F32)
          for r, rows in enumerate(groups)]
    gain1 = mod_ref[0, 2:3, :] * g_post_ref[...]
    gain2 = g_pre2_ref[...] * (1.0 + mod_ref[0, 4:5, :])
    gain3 = mod_ref[0, 5:6, :] * g_post2_ref[...]
    for rows, y in zip(groups, ys):
        x1 = x_ref[0, rows, :] + _rms(y) * gain1
        hb = (_rms(x1) * gain2 + mod_ref[0, 3:4, :]).astype(BF16)
        y2 = jnp.zeros_like(x1)
        for c in range(wup_ref.shape[1] // ff_chunk):
            cols = slice(c * ff_chunk, (c + 1) * ff_chunk)
            up = jnp.dot(hb, wup_ref[:, cols], preferred_element_type=F32)
            act = jnp.square(jnp.maximum(up, 0.0)).astype(BF16)
            y2 = y2 + jnp.dot(act, wdn_ref[cols, :], preferred_element_type=F32)
        o_ref[0, rows, :] = x1 + _rms(y2) * gain3


def _out_mlp(x, attn, pool, mod3, g_post, g_pre2, g_post2, wo, wup, wdn, *, tm, sub_rows,
             ff_chunk):
    B, S, D = x.shape
    A, P, FF = attn.shape[4], pool.shape[2], wup.shape[1]
    assert attn.shape[1:4] == (S // tm, tm // sub_rows, sub_rows)
    const = lambda *shape: pl.BlockSpec(shape, lambda b, i: (0,) * len(shape),
                                        pipeline_mode=pl.Buffered(1))
    return pl.pallas_call(
        functools.partial(_out_mlp_kernel, sub_rows=sub_rows, ff_chunk=ff_chunk),
        grid=(B, S // tm),
        in_specs=[
            pl.BlockSpec((1, tm, D), lambda b, i: (b, i, 0)),
            pl.BlockSpec((1, 1) + attn.shape[2:], lambda b, i: (b, i, 0, 0, 0)),
            pl.BlockSpec((1, tm, P), lambda b, i: (b, i, 0)),
            pl.BlockSpec((1, 6, D), lambda b, i: (b, 0, 0)),
            const(1, D), const(1, D), const(1, D),
            const(A + P, D), const(D, FF), const(FF, D),
        ],
        out_specs=pl.BlockSpec((1, tm, D), lambda b, i: (b, i, 0)),
        out_shape=jax.ShapeDtypeStruct((B, S, D), F32),
        compiler_params=pltpu.CompilerParams(dimension_semantics=("parallel", "parallel"),
                                             vmem_limit_bytes=VMEM_LIMIT),
        name="out_mlp",
    )(x, attn, pool, mod3, g_post, g_pre2, g_post2, wo, wup, wdn)


def _rope_tables(S):
    inv_freq = 1.0 / (ROPE_THETA ** (np.arange(HALF, dtype=np.float64) * (2.0 / HEAD_DIM)))
    ang = np.arange(S, dtype=np.float64)[:, None] * inv_freq[None, :]
    cos, sin = np.cos(ang).astype(np.float32), np.sin(ang).astype(np.float32)
    zero = np.zeros_like(sin)
    reps = LANES // HEAD_DIM
    cosn = np.tile(np.concatenate([cos, cos], axis=1), (1, reps))
    sina = np.tile(np.concatenate([-sin, zero], axis=1), (1, reps))
    sinb = np.tile(np.concatenate([zero, sin], axis=1), (1, reps))
    return tuple(jnp.asarray(t) for t in (cosn, sina, sinb, np.ascontiguousarray(cos.T),
                                          np.ascontiguousarray(sin.T)))


def kernel(x, c, w_ada, b_ada, g_mix_pre, g_mix_post, w_in, w_pool, pool_scale, w_out,
           g_mlp_pre, g_mlp_post, w_up, w_down):
    B, S, D = x.shape
    depth = w_ada.shape[0]
    P = pool_scale.shape[1]
    A = w_out.shape[1] - P
    tables = _rope_tables(S)
    for l in range(depth):
        mod3 = _adaln_mod(c, w_ada[l], b_ada[l]).reshape(B, 6, D)
        w_in_b = w_in[l].astype(BF16)
        wqT = w_in_b[:, :A].T
        wk = w_in_b[:, A:2 * A]
        wvT = w_in_b[:, 2 * A:3 * A].T
        wu = w_in_b[:, 3 * A:]
        qT, k, vT, pool_in, km = _mix_in(x, mod3, g_mix_pre[l][None, :], wqT, wk, wvT, wu, tables,
                                         tm=1024, sub_rows=256)
        wo = jnp.concatenate([w_out[l][:A], _fold_pool(w_pool[l], pool_scale[l][None, :],
                                                      w_out[l], A)], axis=0).astype(BF16)
        attn = _moba_attn(qT, k, vT, km, rows=2 if B % 2 == 0 else 1)
        x = _out_mlp(x, attn, pool_in, mod3, g_mix_post[l][None, :], g_mlp_pre[l][None, :],
                     g_mlp_post[l][None, :], wo, w_up[l].astype(BF16),
                     w_down[l].astype(BF16), tm=S // 2, sub_rows=MOBA_BLOCK, ff_chunk=1024)
    return x
```

```python
import functools

import jax
import jax.numpy as jnp
import numpy as np
from jax import lax
from jax.experimental import pallas as pl
from jax.experimental.pallas import tpu as pltpu

F32 = jnp.float32
BF16 = jnp.bfloat16

HEAD_DIM = 64
HALF = HEAD_DIM // 2
MOBA_BLOCK = 256
MOBA_TOPK = 3
POOL_WINDOWS = (2, 4, 8, 16)
POOL_GROUP = 128
MAX_WINDOW = max(POOL_WINDOWS)
POOL_CHUNK = 128
ROPE_THETA = 10000.0
NORM_EPS = 1e-6
Q_SCALE = HEAD_DIM ** -0.5 * 1.4426950408889634
BF16_ROWS = 16
V_ROWS = HEAD_DIM + BF16_ROWS
LANES = 128
PAIR = LANES // HEAD_DIM
VMEM_LIMIT = 56 * 1024 * 1024

NT_DIMS = (((1,), (1,)), ((), ()))


def _rms(x):
    return x * lax.rsqrt(jnp.mean(x * x, axis=-1, keepdims=True) + NORM_EPS)


def _mod_kernel(c_ref, w_ref, b_ref, o_ref):
    c = c_ref[...]
    act = c * jax.nn.sigmoid(c)
    n = act.shape[0]
    a_hi = act.astype(BF16)
    a_lo = (act - a_hi.astype(F32)).astype(BF16)
    w = w_ref[...]
    w_hi = w.astype(BF16)
    w_lo = (w - w_hi.astype(F32)).astype(BF16)
    both = jnp.dot(jnp.concatenate([a_hi, a_lo], axis=0), w_hi, preferred_element_type=F32)
    o_ref[...] = (both[:n] + both[n:] + jnp.dot(a_hi, w_lo, preferred_element_type=F32)
                  + b_ref[...])


def _adaln_mod(c, w_ada, b_ada):
    B, D = c.shape
    n_out = w_ada.shape[1]
    return pl.pallas_call(
        _mod_kernel,
        grid=(n_out // D,),
        in_specs=[pl.BlockSpec((B, D), lambda n: (0, 0)),
                  pl.BlockSpec((D, D), lambda n: (0, n)),
                  pl.BlockSpec((1, D), lambda n: (0, n))],
        out_specs=pl.BlockSpec((B, D), lambda n: (0, n)),
        out_shape=jax.ShapeDtypeStruct((B, n_out), F32),
        name="adaln_mod",
    )(c, w_ada, b_ada.reshape(1, n_out))


def _fold_pool_kernel(wpool_ref, pscale_ref, wo_ref, o_ref):
    for g in range(wpool_ref.shape[0]):
        cols = slice(g * POOL_GROUP, (g + 1) * POOL_GROUP)
        o_ref[cols, :] = jnp.dot(wpool_ref[g] * pscale_ref[:, cols], wo_ref[cols, :],
                                 precision=lax.Precision.HIGHEST, preferred_element_type=F32)


def _fold_pool(w_pool, pool_scale, w_out, attn_width):
    n_groups, P, D = w_pool.shape[0], pool_scale.shape[1], w_out.shape[1]
    assert attn_width % P == 0 and n_groups * POOL_GROUP == P
    return pl.pallas_call(
        _fold_pool_kernel,
        grid=(1,),
        in_specs=[pl.BlockSpec((n_groups, POOL_GROUP, POOL_GROUP), lambda n: (0, 0, 0)),
                  pl.BlockSpec((1, P), lambda n: (0, 0)),
                  pl.BlockSpec((P, D), lambda n: (attn_width // P, 0))],
        out_specs=pl.BlockSpec((P, D), lambda n: (0, 0)),
        out_shape=jax.ShapeDtypeStruct((P, D), F32),
        name="fold_pool",
    )(w_pool, pool_scale, w_out)


def _mix_in_kernel(x_ref, mod_ref, g_ref, wqT_ref, wk_ref, wvT_ref, wu_ref,
                   cosn_ref, sina_ref, sinb_ref, cost_ref, sint_ref,
                   qT_ref, k_ref, vT_ref, pool_ref, km_ref,
                   ext_ref, *, tm, sub_rows, n_heads):
    i = pl.program_id(1)
    blocks_per_group = sub_rows // MOBA_BLOCK

    @pl.when(i == 0)
    def _():
        ext_ref[:MAX_WINDOW, :] = jnp.zeros((MAX_WINDOW, ext_ref.shape[1]), F32)

    groups = [slice(r * sub_rows, (r + 1) * sub_rows) for r in range(tm // sub_rows)]
    gain = g_ref[...] * (1.0 + mod_ref[0, 1:2, :])
    hbs = []
    for rows in groups:
        x = x_ref[0, rows, :]
        hbs.append((_rms(x) * gain + mod_ref[0, 0:1, :]).astype(BF16))
    for gr, (rows, hb) in enumerate(zip(groups, hbs)):
        _mix_in_group(i * tm + gr * sub_rows, i * (tm // MOBA_BLOCK) + gr * blocks_per_group,
                      rows, gr * blocks_per_group, hb,
                      wqT_ref, wk_ref, wvT_ref, wu_ref,
                      cosn_ref, sina_ref, sinb_ref, cost_ref, sint_ref,
                      qT_ref, k_ref, vT_ref, pool_ref, km_ref, ext_ref, n_heads=n_heads)


def _mix_in_group(pos0, blk0, rows, tile_blk0, hb,
                  wqT_ref, wk_ref, wvT_ref, wu_ref,
                  cosn_ref, sina_ref, sinb_ref, cost_ref, sint_ref,
                  qT_ref, k_ref, vT_ref, pool_ref, km_ref, ext_ref, *, n_heads):
    tg = hb.shape[0]
    blocks_per_group = tg // MOBA_BLOCK
    k = jnp.dot(hb, wk_ref[...], preferred_element_type=F32)
    u = jnp.dot(hb, wu_ref[...], preferred_element_type=F32)
    qT = lax.dot_general(wqT_ref[...], hb, NT_DIMS, preferred_element_type=F32)
    vT = lax.dot_general(wvT_ref[...], hb, NT_DIMS, preferred_element_type=F32)

    cosn, sina, sinb = cosn_ref[rows, :], sina_ref[rows, :], sinb_ref[rows, :]
    k_chunks = []
    for c in range(k.shape[1] // LANES):
        kc = k[:, c * LANES:(c + 1) * LANES]
        kr = (kc * cosn + pltpu.roll(kc, LANES - HALF, axis=1) * sina
              + pltpu.roll(kc, HALF, axis=1) * sinb)
        k_chunks.append(kr)
        k_ref[0, rows, c * LANES:(c + 1) * LANES] = kr.astype(BF16)

    for t in range(blocks_per_group):
        blk = slice(t * MOBA_BLOCK, (t + 1) * MOBA_BLOCK)
        mean = jnp.concatenate([jnp.sum(kr[blk], axis=0, keepdims=True) for kr in k_chunks],
                               axis=1) * (1.0 / MOBA_BLOCK)
        km_ref[0, pl.ds(blk0 + t, 1), :] = mean

    cos_t, sin_t = cost_ref[:, rows], sint_ref[:, rows]
    q_parts = []
    for hd in range(n_heads):
        x1 = qT[hd * HEAD_DIM:hd * HEAD_DIM + HALF]
        x2 = qT[hd * HEAD_DIM + HALF:(hd + 1) * HEAD_DIM]
        q_parts.append(x1 * cos_t - x2 * sin_t)
        q_parts.append(x2 * cos_t + x1 * sin_t)
    qs = (jnp.concatenate(q_parts, axis=0) * Q_SCALE).astype(BF16)
    for t in range(blocks_per_group):
        qT_ref[0, tile_blk0 + t] = qs[:, t * MOBA_BLOCK:(t + 1) * MOBA_BLOCK]
    ones_rows = jnp.where(lax.broadcasted_iota(jnp.int32, (V_ROWS - HEAD_DIM, tg), 0) == 0, 1.0, 0.0)
    v_parts = []
    for hd in range(n_heads):
        v_parts += [vT[hd * HEAD_DIM:(hd + 1) * HEAD_DIM], ones_rows]
    v_aug = jnp.concatenate(v_parts, axis=0).astype(BF16)
    for t in range(blocks_per_group):
        vT_ref[0, tile_blk0 + t] = v_aug[:, t * MOBA_BLOCK:(t + 1) * MOBA_BLOCK]

    ext_ref[MAX_WINDOW:, :] = u
    for r0 in range(0, tg, POOL_CHUNK):
        pos1 = pos0 + r0 + lax.broadcasted_iota(jnp.int32, (POOL_CHUNK, 1), 0) + 1
        for gi, w in enumerate(POOL_WINDOWS):
            cols = slice(gi * POOL_GROUP, (gi + 1) * POOL_GROUP)
            win = ext_ref[r0:r0 + MAX_WINDOW + POOL_CHUNK, cols]
            token = win[MAX_WINDOW:, :]
            span = 1
            while span < w:
                win = win + pltpu.roll(win, span, axis=0)
                span *= 2
            count = jnp.minimum(pos1, w).astype(F32)
            pool_ref[0, rows.start + r0:rows.start + r0 + POOL_CHUNK, cols] = (
                win[MAX_WINDOW:, :] / count - token).astype(BF16)
    ext_ref[:MAX_WINDOW, :] = ext_ref[tg:, :]


def _mix_in(x, mod3, g_pre, wqT, wk, wvT, wu, tables, *, tm, sub_rows):
    B, S, D = x.shape
    A = wk.shape[1]
    P = wu.shape[1]
    n_heads = A // HEAD_DIM
    n_blocks = S // MOBA_BLOCK
    bpt = tm // MOBA_BLOCK
    cosn, sina, sinb, cost, sint = tables
    const = lambda *shape: pl.BlockSpec(shape, lambda b, i: (0,) * len(shape),
                                        pipeline_mode=pl.Buffered(1))
    kern = functools.partial(_mix_in_kernel, tm=tm, sub_rows=sub_rows, n_heads=n_heads)
    return pl.pallas_call(
        kern,
        grid=(B, S // tm),
        in_specs=[
            pl.BlockSpec((1, tm, D), lambda b, i: (b, i, 0)),
            pl.BlockSpec((1, 6, D), lambda b, i: (b, 0, 0)),
            const(1, D),
            const(A, D), const(D, A), const(A, D), const(D, P),
            pl.BlockSpec((tm, LANES), lambda b, i: (i, 0)),
            pl.BlockSpec((tm, LANES), lambda b, i: (i, 0)),
            pl.BlockSpec((tm, LANES), lambda b, i: (i, 0)),
            pl.BlockSpec((HALF, tm), lambda b, i: (0, i)),
            pl.BlockSpec((HALF, tm), lambda b, i: (0, i)),
        ],
        out_specs=[
            pl.BlockSpec((1, bpt, A, MOBA_BLOCK), lambda b, i: (b, i, 0, 0)),
            pl.BlockSpec((1, tm, A), lambda b, i: (b, i, 0)),
            pl.BlockSpec((1, bpt, n_heads * V_ROWS, MOBA_BLOCK), lambda b, i: (b, i, 0, 0)),
            pl.BlockSpec((1, tm, P), lambda b, i: (b, i, 0)),
            pl.BlockSpec((1, n_blocks, A), lambda b, i: (b, 0, 0)),
        ],
        out_shape=[
            jax.ShapeDtypeStruct((B, n_blocks, A, MOBA_BLOCK), BF16),
            jax.ShapeDtypeStruct((B, S, A), BF16),
            jax.ShapeDtypeStruct((B, n_blocks, n_heads * V_ROWS, MOBA_BLOCK), BF16),
            jax.ShapeDtypeStruct((B, S, P), BF16),
            jax.ShapeDtypeStruct((B, n_blocks, A), F32),
        ],
        scratch_shapes=[pltpu.VMEM((MAX_WINDOW + sub_rows, P), F32)],
        compiler_params=pltpu.CompilerParams(dimension_semantics=("parallel", "arbitrary"),
                                             vmem_limit_bytes=VMEM_LIMIT),
        name="mix_in",
    )(x, mod3, g_pre, wqT, wk, wvT, wu, cosn, sina, sinb, cost, sint)


def _attn_kernel(qa_ref, qb_ref, k_ref, vT_ref, km_ref, o_ref,
                 qz_ref, kmx_ref, sel_ref, m_ref, acc_ref,
                 s0_ref, alpha0_ref, shift0_ref, s1_ref, alpha1_ref, shift1_ref,
                 *, rows, n_heads, n_blocks):
    p = pl.program_id(1)
    q_blocks = (p, n_blocks - 1 - p)
    tq = MOBA_BLOCK
    items = [(r, hd) for r in range(rows) for hd in range(n_heads)]
    n_items = len(items)
    n_chunks = n_heads // PAIR
    halves = (list(enumerate(items))[:n_items // 2], list(enumerate(items))[n_items // 2:])
    zeros = jnp.zeros((HEAD_DIM, tq), BF16)
    for w, q_ref in enumerate((qa_ref, qb_ref)):
        for it, (r, hd) in enumerate(items):
            qh = q_ref[r, 0, hd * HEAD_DIM:(hd + 1) * HEAD_DIM, :]
            qz_ref[w * n_items + it] = jnp.concatenate(
                [qh, zeros] if hd % PAIR == 0 else [zeros, qh], axis=0)
    for r in range(rows):
        for c in range(n_chunks):
            km = km_ref[r, :, c * LANES:(c + 1) * LANES]
            km_hi = km.astype(BF16).astype(F32)
            kmx_ref[r * n_chunks + c] = jnp.concatenate([km_hi, km - km_hi], axis=0).astype(BF16)
    acc_ref[...] = jnp.zeros_like(acc_ref)
    buffers = ((s0_ref, alpha0_ref, shift0_ref), (s1_ref, alpha1_ref, shift1_ref))
    row_j = lax.broadcasted_iota(jnp.int32, (n_blocks, tq), 0)

    max_past = (n_blocks // 2 - 1, n_blocks - 1)

    def select_blocks(gate, n_valid):
        valid = row_j < n_valid
        if gate is None:
            return jnp.where(valid, 0.0, -jnp.inf)
        g = jnp.where(valid, gate, -jnp.inf)
        rank = jnp.zeros((n_blocks, tq), jnp.int32)
        for j in range(n_blocks):
            gj = jnp.broadcast_to(g[j:j + 1, :], g.shape)
            ahead = (gj > g) | ((gj == g) & (row_j > j))
            rank = rank + jnp.where(ahead, 1, 0)
        return jnp.where(valid & (rank < MOBA_TOPK), 0.0, -jnp.inf)

    def pass1(entries, w, j, own, dst):
        s_ref, alpha_ref, shift_ref = dst
        keys = pl.ds(pl.multiple_of(j * tq, tq), tq)
        qz_w, m_w = qz_ref.at[pl.ds(w * n_items, n_items)], m_ref.at[pl.ds(w * n_items, n_items)]
        for it, (r, hd) in entries:
            c = hd // PAIR
            kp = k_ref[r, keys, c * LANES:(c + 1) * LANES]
            gated = own and max_past[w] > MOBA_TOPK
            if gated:
                kp = jnp.concatenate([kp, kmx_ref[r * n_chunks + c]], axis=0)
            s = jnp.dot(kp, qz_w[it], preferred_element_type=F32)
            if own:
                gate = s[tq:tq + n_blocks] + s[tq + n_blocks:] if gated else None
                sel_ref[w, r, hd * n_blocks:(hd + 1) * n_blocks, :] = select_blocks(gate, j)
                key_i = lax.broadcasted_iota(jnp.int32, (tq, tq), 0)
                qry_i = lax.broadcasted_iota(jnp.int32, (tq, tq), 1)
                s = jnp.where(key_i <= qry_i, s[:tq], -jnp.inf)
            s_ref[it] = s
            cmax = jnp.max(s, axis=0, keepdims=True)
            if own:
                m_w[it] = cmax
                alpha_ref[it] = jnp.zeros_like(cmax)
                shift_ref[it] = cmax
            else:
                bias = sel_ref[w, r, pl.ds(hd * n_blocks + j, 1), :]
                m_old = m_w[it]
                m_new = jnp.maximum(m_old, cmax + bias)
                m_w[it] = m_new
                alpha_ref[it] = jnp.exp2(m_old - m_new)
                shift_ref[it] = m_new - bias

    def pass2(entries, w, j, src):
        s_ref, alpha_ref, shift_ref = src
        acc_w = acc_ref.at[pl.ds(w * n_items, n_items)]
        for it, (r, hd) in entries:
            prob = jnp.exp2(s_ref[it] - shift_ref[it]).astype(BF16)
            pv = jnp.dot(vT_ref[r, j, hd * V_ROWS:(hd + 1) * V_ROWS, :], prob,
                         preferred_element_type=F32)
            acc_w[it] = alpha_ref[it] * acc_w[it] + pv

    n_elems = n_blocks + 1

    def elem(n):
        if isinstance(n, int) and n < 2:
            return n, q_blocks[n], True
        e = n - 2
        first = e < q_blocks[0]
        return jnp.where(first, 0, 1), jnp.where(first, e, e - q_blocks[0]), False

    def turn(t, parity):
        for half, n in ((0, t), (1, t - 1)):
            static = isinstance(n, int)
            ahead = elem(n) if not static or 0 <= n < n_elems else None
            behind = elem(n - 1) if not static or 1 <= n <= n_elems else None
            for entry in halves[half]:
                if ahead is not None:
                    pass1([entry], *ahead, buffers[(parity + half) % 2])
                if behind is not None:
                    pass2([entry], *behind[:2], buffers[(parity + 1 + half) % 2])

    first_rolled, n_rolled = 4, 2 * ((n_elems - 4) // 2)
    for t in range(first_rolled):
        turn(t, t % 2)

    def two_turns(u, carry):
        t = first_rolled + 2 * u
        turn(t, first_rolled % 2)
        turn(t + 1, (first_rolled + 1) % 2)
        return carry

    lax.fori_loop(0, n_rolled // 2, two_turns, 0)
    for t in range(first_rolled + n_rolled, n_elems + 2):
        turn(t, t % 2)

    for w in range(2):
        for r in range(rows):
            for c in range(n_heads // PAIR):
                pair = [acc_ref[it, :HEAD_DIM] * (1.0 / acc_ref[it, HEAD_DIM:HEAD_DIM + 1])
                        for it in range(w * n_items + r * n_heads + c * PAIR,
                                        w * n_items + r * n_heads + (c + 1) * PAIR)]
                o_ref[r, w, 0, :, c * LANES:(c + 1) * LANES] = (
                    jnp.concatenate(pair, axis=0).T.astype(BF16))


def _moba_attn(qT, k, vT, km, *, rows):
    B, n_blocks, A, _ = qT.shape
    S = n_blocks * MOBA_BLOCK
    n_heads = A // HEAD_DIM
    assert n_blocks % 2 == 0 and (n_blocks - 1) % 2 == 1
    n_items = rows * n_heads
    kern = functools.partial(_attn_kernel, rows=rows, n_heads=n_heads, n_blocks=n_blocks)
    stage = [pltpu.VMEM((n_items, MOBA_BLOCK, MOBA_BLOCK), F32),
             pltpu.VMEM((n_items, 1, MOBA_BLOCK), F32),
             pltpu.VMEM((n_items, 1, MOBA_BLOCK), F32)]
    last = n_blocks - 1
    return pl.pallas_call(
        kern,
        grid=(B // rows, n_blocks // 2),
        in_specs=[
            pl.BlockSpec((rows, 1, A, MOBA_BLOCK), lambda b, p: (b, p, 0, 0)),
            pl.BlockSpec((rows, 1, A, MOBA_BLOCK), lambda b, p: (b, last - p, 0, 0)),
            pl.BlockSpec((rows, S, A), lambda b, p: (b, 0, 0)),
            pl.BlockSpec((rows, n_blocks, n_heads * V_ROWS, MOBA_BLOCK), lambda b, p: (b, 0, 0, 0)),
            pl.BlockSpec((rows, n_blocks, A), lambda b, p: (b, 0, 0)),
        ],
        out_specs=pl.BlockSpec((rows, 2, 1, MOBA_BLOCK, A), lambda b, p: (b, 0, p, 0, 0)),
        out_shape=jax.ShapeDtypeStruct((B, 2, n_blocks // 2, MOBA_BLOCK, A), BF16),
        scratch_shapes=[
            pltpu.VMEM((2 * n_items, LANES, MOBA_BLOCK), BF16),
            pltpu.VMEM((rows * A // LANES, 2 * n_blocks, LANES), BF16),
            pltpu.VMEM((2, rows, n_heads * n_blocks, MOBA_BLOCK), F32),
            pltpu.VMEM((2 * n_items, 1, MOBA_BLOCK), F32),
            pltpu.VMEM((2 * n_items, V_ROWS, MOBA_BLOCK), F32),
        ] + stage + stage,
        compiler_params=pltpu.CompilerParams(dimension_semantics=("parallel", "arbitrary"),
                                             vmem_limit_bytes=VMEM_LIMIT),
        name="moba_attn",
    )(qT, qT, k, vT, km)


def _out_mlp_kernel(x_ref, attn_ref, pool_ref, mod_ref, g_post_ref, g_pre2_ref, g_post2_ref,
                    wo_ref, wup_ref, wdn_ref, o_ref, *, sub_rows, ff_chunk):
    A = attn_ref.shape[4]
    n_groups = x_ref.shape[1] // sub_rows
    half = pl.program_id(1)
    groups = [slice(r * sub_rows, (r + 1) * sub_rows) for r in range(n_groups)]
    ys = [jnp.dot(attn_ref[0, 0, jnp.where(half == 0, r, n_groups - 1 - r)], wo_ref[:A, :],
                  preferred_element_type=F32)
          + jnp.dot(pool_ref[0, rows, :], wo_ref[A:, :], preferred_element_type=F32)
          for r, rows in enumerate(groups)]
    gain1 = mod_ref[0, 2:3, :] * g_post_ref[...]
    gain2 = g_pre2_ref[...] * (1.0 + mod_ref[0, 4:5, :])
    gain3 = mod_ref[0, 5:6, :] * g_post2_ref[...]
    for rows, y in zip(groups, ys):
        x1 = x_ref[0, rows, :] + _rms(y) * gain1
        hb = (_rms(x1) * gain2 + mod_ref[0, 3:4, :]).astype(BF16)
        y2 = jnp.zeros_like(x1)
        for c in range(wup_ref.shape[1] // ff_chunk):
            cols = slice(c * ff_chunk, (c + 1) * ff_chunk)
            up = jnp.dot(hb, wup_ref[:, cols], preferred_element_type=F32)
            act = jnp.square(jnp.maximum(up, 0.0)).astype(BF16)
            y2 = y2 + jnp.dot(act, wdn_ref[cols, :], preferred_element_type=F32)
        o_ref[0, rows, :] = x1 + _rms(y2) * gain3


def _out_mlp(x, attn, pool, mod3, g_post, g_pre2, g_post2, wo, wup, wdn, *, tm, sub_rows,
             ff_chunk):
    B, S, D = x.shape
    A, P, FF = attn.shape[4], pool.shape[2], wup.shape[1]
    assert attn.shape[1:4] == (S // tm, tm // sub_rows, sub_rows)
    const = lambda *shape: pl.BlockSpec(shape, lambda b, i: (0,) * len(shape),
                                        pipeline_mode=pl.Buffered(1))
    return pl.pallas_call(
        functools.partial(_out_mlp_kernel, sub_rows=sub_rows, ff_chunk=ff_chunk),
        grid=(B, S // tm),
        in_specs=[
            pl.BlockSpec((1, tm, D), lambda b, i: (b, i, 0)),
            pl.BlockSpec((1, 1) + attn.shape[2:], lambda b, i: (b, i, 0, 0, 0)),
            pl.BlockSpec((1, tm, P), lambda b, i: (b, i, 0)),
            pl.BlockSpec((1, 6, D), lambda b, i: (b, 0, 0)),
            const(1, D), const(1, D), const(1, D),
            const(A + P, D), const(D, FF), const(FF, D),
        ],
        out_specs=pl.BlockSpec((1, tm, D), lambda b, i: (b, i, 0)),
        out_shape=jax.ShapeDtypeStruct((B, S, D), F32),
        compiler_params=pltpu.CompilerParams(dimension_semantics=("parallel", "parallel"),
                                             vmem_limit_bytes=VMEM_LIMIT),
        name="out_mlp",
    )(x, attn, pool, mod3, g_post, g_pre2, g_post2, wo, wup, wdn)


def _rope_tables(S):
    inv_freq = 1.0 / (ROPE_THETA ** (np.arange(HALF, dtype=np.float64) * (2.0 / HEAD_DIM)))
    ang = np.arange(S, dtype=np.float64)[:, None] * inv_freq[None, :]
    cos, sin = np.cos(ang).astype(np.float32), np.sin(ang).astype(np.float32)
    zero = np.zeros_like(sin)
    reps = LANES // HEAD_DIM
    cosn = np.tile(np.concatenate([cos, cos], axis=1), (1, reps))
    sina = np.tile(np.concatenate([-sin, zero], axis=1), (1, reps))
    sinb = np.tile(np.concatenate([zero, sin], axis=1), (1, reps))
    return tuple(jnp.asarray(t) for t in (cosn, sina, sinb, np.ascontiguousarray(cos.T),
                                          np.ascontiguousarray(sin.T)))


def kernel(x, c, w_ada, b_ada, g_mix_pre, g_mix_post, w_in, w_pool, pool_scale, w_out,
           g_mlp_pre, g_mlp_post, w_up, w_down):
    B, S, D = x.shape
    depth = w_ada.shape[0]
    P = pool_scale.shape[1]
    A = w_out.shape[1] - P
    tables = _rope_tables(S)
    for l in range(depth):
        mod3 = _adaln_mod(c, w_ada[l], b_ada[l]).reshape(B, 6, D)
        w_in_b = w_in[l].astype(BF16)
        wqT = w_in_b[:, :A].T
        wk = w_in_b[:, A:2 * A]
        wvT = w_in_b[:, 2 * A:3 * A].T
        wu = w_in_b[:, 3 * A:]
        qT, k, vT, pool_in, km = _mix_in(x, mod3, g_mix_pre[l][None, :], wqT, wk, wvT, wu, tables,
                                         tm=1024, sub_rows=256)
        wo = jnp.concatenate([w_out[l][:A], _fold_pool(w_pool[l], pool_scale[l][None, :],
                                                      w_out[l], A)], axis=0).astype(BF16)
        attn = _moba_attn(qT, k, vT, km, rows=2 if B % 2 == 0 else 1)
        x = _out_mlp(x, attn, pool_in, mod3, g_mix_post[l][None, :], g_mlp_pre[l][None, :],
                     g_mlp_post[l][None, :], wo, w_up[l].astype(BF16),
                     w_down[l].astype(BF16), tm=S // 2, sub_rows=MOBA_BLOCK, ff_chunk=1024)
    return x
```

```python
import functools

import jax
import jax.numpy as jnp
import numpy as np
from jax import lax
from jax.experimental import pallas as pl
from jax.experimental.pallas import tpu as pltpu

F32 = jnp.float32
BF16 = jnp.bfloat16

HEAD_DIM = 64
HALF = HEAD_DIM // 2
MOBA_BLOCK = 256
MOBA_TOPK = 3
POOL_WINDOWS = (2, 4, 8, 16)
POOL_GROUP = 128
MAX_WINDOW = max(POOL_WINDOWS)
POOL_CHUNK = 128
ROPE_THETA = 10000.0
NORM_EPS = 1e-6
Q_SCALE = HEAD_DIM ** -0.5 * 1.4426950408889634
BF16_ROWS = 16
V_ROWS = HEAD_DIM + BF16_ROWS
LANES = 128
PAIR = LANES // HEAD_DIM
VMEM_LIMIT = 56 * 1024 * 1024

NT_DIMS = (((1,), (1,)), ((), ()))


def _rms(x):
    return x * lax.rsqrt(jnp.mean(x * x, axis=-1, keepdims=True) + NORM_EPS)


def _mod_kernel(c_ref, w_ref, b_ref, o_ref):
    c = c_ref[...]
    act = c * jax.nn.sigmoid(c)
    n = act.shape[0]
    a_hi = act.astype(BF16)
    a_lo = (act - a_hi.astype(F32)).astype(BF16)
    w = w_ref[...]
    w_hi = w.astype(BF16)
    w_lo = (w - w_hi.astype(F32)).astype(BF16)
    both = jnp.dot(jnp.concatenate([a_hi, a_lo], axis=0), w_hi, preferred_element_type=F32)
    o_ref[...] = (both[:n] + both[n:] + jnp.dot(a_hi, w_lo, preferred_element_type=F32)
                  + b_ref[...])


def _adaln_mod(c, w_ada, b_ada):
    B, D = c.shape
    n_out = w_ada.shape[1]
    return pl.pallas_call(
        _mod_kernel,
        grid=(n_out // D,),
        in_specs=[pl.BlockSpec((B, D), lambda n: (0, 0)),
                  pl.BlockSpec((D, D), lambda n: (0, n)),
                  pl.BlockSpec((1, D), lambda n: (0, n))],
        out_specs=pl.BlockSpec((B, D), lambda n: (0, n)),
        out_shape=jax.ShapeDtypeStruct((B, n_out), F32),
        name="adaln_mod",
    )(c, w_ada, b_ada.reshape(1, n_out))


def _fold_pool_kernel(wpool_ref, pscale_ref, wo_ref, o_ref):
    for g in range(wpool_ref.shape[0]):
        cols = slice(g * POOL_GROUP, (g + 1) * POOL_GROUP)
        o_ref[cols, :] = jnp.dot(wpool_ref[g] * pscale_ref[:, cols], wo_ref[cols, :],
                                 precision=lax.Precision.HIGHEST, preferred_element_type=F32)


def _fold_pool(w_pool, pool_scale, w_out, attn_width):
    n_groups, P, D = w_pool.shape[0], pool_scale.shape[1], w_out.shape[1]
    assert attn_width % P == 0 and n_groups * POOL_GROUP == P
    return pl.pallas_call(
        _fold_pool_kernel,
        grid=(1,),
        in_specs=[pl.BlockSpec((n_groups, POOL_GROUP, POOL_GROUP), lambda n: (0, 0, 0)),
                  pl.BlockSpec((1, P), lambda n: (0, 0)),
                  pl.BlockSpec((P, D), lambda n: (attn_width // P, 0))],
        out_specs=pl.BlockSpec((P, D), lambda n: (0, 0)),
        out_shape=jax.ShapeDtypeStruct((P, D), F32),
        name="fold_pool",
    )(w_pool, pool_scale, w_out)


def _mix_in_kernel(x_ref, mod_ref, g_ref, wqT_ref, wk_ref, wvT_ref, wu_ref,
                   cosn_ref, sina_ref, sinb_ref, cost_ref, sint_ref,
                   qT_ref, k_ref, vT_ref, pool_ref, km_ref,
                   ext_ref, *, tm, sub_rows, n_heads):
    i = pl.program_id(1)
    blocks_per_group = sub_rows // MOBA_BLOCK

    @pl.when(i == 0)
    def _():
        ext_ref[:MAX_WINDOW, :] = jnp.zeros((MAX_WINDOW, ext_ref.shape[1]), F32)

    groups = [slice(r * sub_rows, (r + 1) * sub_rows) for r in range(tm // sub_rows)]
    gain = g_ref[...] * (1.0 + mod_ref[0, 1:2, :])
    hbs = []
    for rows in groups:
        x = x_ref[0, rows, :]
        hbs.append((_rms(x) * gain + mod_ref[0, 0:1, :]).astype(BF16))
    for gr, (rows, hb) in enumerate(zip(groups, hbs)):
        _mix_in_group(i * tm + gr * sub_rows, i * (tm // MOBA_BLOCK) + gr * blocks_per_group,
                      rows, gr * blocks_per_group, hb,
                      wqT_ref, wk_ref, wvT_ref, wu_ref,
                      cosn_ref, sina_ref, sinb_ref, cost_ref, sint_ref,
                      qT_ref, k_ref, vT_ref, pool_ref, km_ref, ext_ref, n_heads=n_heads)


def _mix_in_group(pos0, blk0, rows, tile_blk0, hb,
                  wqT_ref, wk_ref, wvT_ref, wu_ref,
                  cosn_ref, sina_ref, sinb_ref, cost_ref, sint_ref,
                  qT_ref, k_ref, vT_ref, pool_ref, km_ref, ext_ref, *, n_heads):
    tg = hb.shape[0]
    blocks_per_group = tg // MOBA_BLOCK
    k = jnp.dot(hb, wk_ref[...], preferred_element_type=F32)
    u = jnp.dot(hb, wu_ref[...], preferred_element_type=F32)
    qT = lax.dot_general(wqT_ref[...], hb, NT_DIMS, preferred_element_type=F32)
    vT = lax.dot_general(wvT_ref[...], hb, NT_DIMS, preferred_element_type=F32)

    cosn, sina, sinb = cosn_ref[rows, :], sina_ref[rows, :], sinb_ref[rows, :]
    k_chunks = []
    for c in range(k.shape[1] // LANES):
        kc = k[:, c * LANES:(c + 1) * LANES]
        kr = (kc * cosn + pltpu.roll(kc, LANES - HALF, axis=1) * sina
              + pltpu.roll(kc, HALF, axis=1) * sinb)
        k_chunks.append(kr)
        k_ref[0, rows, c * LANES:(c + 1) * LANES] = kr.astype(BF16)

    for t in range(blocks_per_group):
        blk = slice(t * MOBA_BLOCK, (t + 1) * MOBA_BLOCK)
        mean = jnp.concatenate([jnp.sum(kr[blk], axis=0, keepdims=True) for kr in k_chunks],
                               axis=1) * (1.0 / MOBA_BLOCK)
        km_ref[0, pl.ds(blk0 + t, 1), :] = mean

    cos_t, sin_t = cost_ref[:, rows], sint_ref[:, rows]
    q_parts = []
    for hd in range(n_heads):
        x1 = qT[hd * HEAD_DIM:hd * HEAD_DIM + HALF]
        x2 = qT[hd * HEAD_DIM + HALF:(hd + 1) * HEAD_DIM]
        q_parts.append(x1 * cos_t - x2 * sin_t)
        q_parts.append(x2 * cos_t + x1 * sin_t)
    qs = (jnp.concatenate(q_parts, axis=0) * Q_SCALE).astype(BF16)
    for t in range(blocks_per_group):
        qT_ref[0, tile_blk0 + t] = qs[:, t * MOBA_BLOCK:(t + 1) * MOBA_BLOCK]
    ones_rows = jnp.where(lax.broadcasted_iota(jnp.int32, (V_ROWS - HEAD_DIM, tg), 0) == 0, 1.0, 0.0)
    v_parts = []
    for hd in range(n_heads):
        v_parts += [vT[hd * HEAD_DIM:(hd + 1) * HEAD_DIM], ones_rows]
    v_aug = jnp.concatenate(v_parts, axis=0).astype(BF16)
    for t in range(blocks_per_group):
        vT_ref[0, tile_blk0 + t] = v_aug[:, t * MOBA_BLOCK:(t + 1) * MOBA_BLOCK]

    ext_ref[MAX_WINDOW:, :] = u
    for r0 in range(0, tg, POOL_CHUNK):
        pos1 = pos0 + r0 + lax.broadcasted_iota(jnp.int32, (POOL_CHUNK, 1), 0) + 1
        for gi, w in enumerate(POOL_WINDOWS):
            cols = slice(gi * POOL_GROUP, (gi + 1) * POOL_GROUP)
            win = ext_ref[r0:r0 + MAX_WINDOW + POOL_CHUNK, cols]
            token = win[MAX_WINDOW:, :]
            span = 1
            while span < w:
                win = win + pltpu.roll(win, span, axis=0)
                span *= 2
            count = jnp.minimum(pos1, w).astype(F32)
            pool_ref[0, rows.start + r0:rows.start + r0 + POOL_CHUNK, cols] = (
                win[MAX_WINDOW:, :] / count - token).astype(BF16)
    ext_ref[:MAX_WINDOW, :] = ext_ref[tg:, :]


def _mix_in(x, mod3, g_pre, wqT, wk, wvT, wu, tables, *, tm, sub_rows):
    B, S, D = x.shape
    A = wk.shape[1]
    P = wu.shape[1]
    n_heads = A // HEAD_DIM
    n_blocks = S // MOBA_BLOCK
    bpt = tm // MOBA_BLOCK
    cosn, sina, sinb, cost, sint = tables
    const = lambda *shape: pl.BlockSpec(shape, lambda b, i: (0,) * len(shape),
                                        pipeline_mode=pl.Buffered(1))
    kern = functools.partial(_mix_in_kernel, tm=tm, sub_rows=sub_rows, n_heads=n_heads)
    return pl.pallas_call(
        kern,
        grid=(B, S // tm),
        in_specs=[
            pl.BlockSpec((1, tm, D), lambda b, i: (b, i, 0)),
            pl.BlockSpec((1, 6, D), lambda b, i: (b, 0, 0)),
            const(1, D),
            const(A, D), const(D, A), const(A, D), const(D, P),
            pl.BlockSpec((tm, LANES), lambda b, i: (i, 0)),
            pl.BlockSpec((tm, LANES), lambda b, i: (i, 0)),
            pl.BlockSpec((tm, LANES), lambda b, i: (i, 0)),
            pl.BlockSpec((HALF, tm), lambda b, i: (0, i)),
            pl.BlockSpec((HALF, tm), lambda b, i: (0, i)),
        ],
        out_specs=[
            pl.BlockSpec((1, bpt, A, MOBA_BLOCK), lambda b, i: (b, i, 0, 0)),
            pl.BlockSpec((1, tm, A), lambda b, i: (b, i, 0)),
            pl.BlockSpec((1, bpt, n_heads * V_ROWS, MOBA_BLOCK), lambda b, i: (b, i, 0, 0)),
            pl.BlockSpec((1, tm, P), lambda b, i: (b, i, 0)),
            pl.BlockSpec((1, n_blocks, A), lambda b, i: (b, 0, 0)),
        ],
        out_shape=[
            jax.ShapeDtypeStruct((B, n_blocks, A, MOBA_BLOCK), BF16),
            jax.ShapeDtypeStruct((B, S, A), BF16),
            jax.ShapeDtypeStruct((B, n_blocks, n_heads * V_ROWS, MOBA_BLOCK), BF16),
            jax.ShapeDtypeStruct((B, S, P), BF16),
            jax.ShapeDtypeStruct((B, n_blocks, A), F32),
        ],
        scratch_shapes=[pltpu.VMEM((MAX_WINDOW + sub_rows, P), F32)],
        compiler_params=pltpu.CompilerParams(dimension_semantics=("parallel", "arbitrary"),
                                             vmem_limit_bytes=VMEM_LIMIT),
        name="mix_in",
    )(x, mod3, g_pre, wqT, wk, wvT, wu, cosn, sina, sinb, cost, sint)


def _attn_kernel(qa_ref, qb_ref, k_ref, vT_ref, km_ref, o_ref,
                 qz_ref, kmx_ref, sel_ref, m_ref, acc_ref,
                 s0_ref, alpha0_ref, shift0_ref, s1_ref, alpha1_ref, shift1_ref,
                 *, rows, n_heads, n_blocks):
    p = pl.program_id(1)
    q_blocks = (p, n_blocks - 1 - p)
    tq = MOBA_BLOCK
    items = [(r, hd) for r in range(rows) for hd in range(n_heads)]
    n_items = len(items)
    n_chunks = n_heads // PAIR
    halves = (list(enumerate(items))[:n_items // 2], list(enumerate(items))[n_items // 2:])
    zeros = jnp.zeros((HEAD_DIM, tq), BF16)
    for w, q_ref in enumerate((qa_ref, qb_ref)):
        for it, (r, hd) in enumerate(items):
            qh = q_ref[r, 0, hd * HEAD_DIM:(hd + 1) * HEAD_DIM, :]
            qz_ref[w * n_items + it] = jnp.concatenate(
                [qh, zeros] if hd % PAIR == 0 else [zeros, qh], axis=0)
    for r in range(rows):
        for c in range(n_chunks):
            km = km_ref[r, :, c * LANES:(c + 1) * LANES]
            km_hi = km.astype(BF16).astype(F32)
            kmx_ref[r * n_chunks + c] = jnp.concatenate([km_hi, km - km_hi], axis=0).astype(BF16)
    acc_ref[...] = jnp.zeros_like(acc_ref)
    buffers = ((s0_ref, alpha0_ref, shift0_ref), (s1_ref, alpha1_ref, shift1_ref))
    row_j = lax.broadcasted_iota(jnp.int32, (n_blocks, tq), 0)

    max_past = (n_blocks // 2 - 1, n_blocks - 1)

    def select_blocks(gate, n_valid):
        valid = row_j < n_valid
        if gate is None:
            return jnp.where(valid, 0.0, -jnp.inf)
        g = jnp.where(valid, gate, -jnp.inf)
        rank = jnp.zeros((n_blocks, tq), jnp.int32)
        for j in range(n_blocks):
            gj = jnp.broadcast_to(g[j:j + 1, :], g.shape)
            ahead = (gj > g) | ((gj == g) & (row_j > j))
            rank = rank + jnp.where(ahead, 1, 0)
        return jnp.where(valid & (rank < MOBA_TOPK), 0.0, -jnp.inf)

    def pass1(entries, w, j, own, dst):
        s_ref, alpha_ref, shift_ref = dst
        keys = pl.ds(pl.multiple_of(j * tq, tq), tq)
        qz_w, m_w = qz_ref.at[pl.ds(w * n_items, n_items)], m_ref.at[pl.ds(w * n_items, n_items)]
        for it, (r, hd) in entries:
            c = hd // PAIR
            kp = k_ref[r, keys, c * LANES:(c + 1) * LANES]
            gated = own and max_past[w] > MOBA_TOPK
            if gated:
                kp = jnp.concatenate([kp, kmx_ref[r * n_chunks + c]], axis=0)
            s = jnp.dot(kp, qz_w[it], preferred_element_type=F32)
            if own:
                gate = s[tq:tq + n_blocks] + s[tq + n_blocks:] if gated else None
                sel_ref[w, r, hd * n_blocks:(hd + 1) * n_blocks, :] = select_blocks(gate, j)
                key_i = lax.broadcasted_iota(jnp.int32, (tq, tq), 0)
                qry_i = lax.broadcasted_iota(jnp.int32, (tq, tq), 1)
                s = jnp.where(key_i <= qry_i, s[:tq], -jnp.inf)
            s_ref[it] = s
            cmax = jnp.max(s, axis=0, keepdims=True)
            if own:
                m_w[it] = cmax
                alpha_ref[it] = jnp.zeros_like(cmax)
                shift_ref[it] = cmax
            else:
                bias = sel_ref[w, r, pl.ds(hd * n_blocks + j, 1), :]
                m_old = m_w[it]
                m_new = jnp.maximum(m_old, cmax + bias)
                m_w[it] = m_new
                alpha_ref[it] = jnp.exp2(m_old - m_new)
                shift_ref[it] = m_new - bias

    def pass2(entries, w, j, src):
        s_ref, alpha_ref, shift_ref = src
        acc_w = acc_ref.at[pl.ds(w * n_items, n_items)]
        for it, (r, hd) in entries:
            prob = jnp.exp2(s_ref[it] - shift_ref[it]).astype(BF16)
            pv = jnp.dot(vT_ref[r, j, hd * V_ROWS:(hd + 1) * V_ROWS, :], prob,
                         preferred_element_type=F32)
            acc_w[it] = alpha_ref[it] * acc_w[it] + pv

    n_elems = n_blocks + 1

    def elem(n):
        if isinstance(n, int) and n < 2:
            return n, q_blocks[n], True
        e = n - 2
        first = e < q_blocks[0]
        return jnp.where(first, 0, 1), jnp.where(first, e, e - q_blocks[0]), False

    def turn(t, parity):
        for half, n in ((0, t), (1, t - 1)):
            static = isinstance(n, int)
            ahead = elem(n) if not static or 0 <= n < n_elems else None
            behind = elem(n - 1) if not static or 1 <= n <= n_elems else None
            for entry in halves[half]:
                if ahead is not None:
                    pass1([entry], *ahead, buffers[(parity + half) % 2])
                if behind is not None:
                    pass2([entry], *behind[:2], buffers[(parity + 1 + half) % 2])

    first_rolled, n_rolled = 4, 2 * ((n_elems - 4) // 2)
    for t in range(first_rolled):
        turn(t, t % 2)

    def two_turns(u, carry):
        t = first_rolled + 2 * u
        turn(t, first_rolled % 2)
        turn(t + 1, (first_rolled + 1) % 2)
        return carry

    lax.fori_loop(0, n_rolled // 2, two_turns, 0)
    for t in range(first_rolled + n_rolled, n_elems + 2):
        turn(t, t % 2)

    for w in range(2):
        for r in range(rows):
            for c in range(n_heads // PAIR):
                pair = [acc_ref[it, :HEAD_DIM] * (1.0 / acc_ref[it, HEAD_DIM:HEAD_DIM + 1])
                        for it in range(w * n_items + r * n_heads + c * PAIR,
                                        w * n_items + r * n_heads + (c + 1) * PAIR)]
                o_ref[r, w, 0, :, c * LANES:(c + 1) * LANES] = (
                    jnp.concatenate(pair, axis=0).T.astype(BF16))


def _moba_attn(qT, k, vT, km, *, rows):
    B, n_blocks, A, _ = qT.shape
    S = n_blocks * MOBA_BLOCK
    n_heads = A // HEAD_DIM
    assert n_blocks % 2 == 0 and (n_blocks - 1) % 2 == 1
    n_items = rows * n_heads
    kern = functools.partial(_attn_kernel, rows=rows, n_heads=n_heads, n_blocks=n_blocks)
    stage = [pltpu.VMEM((n_items, MOBA_BLOCK, MOBA_BLOCK), F32),
             pltpu.VMEM((n_items, 1, MOBA_BLOCK), F32),
             pltpu.VMEM((n_items, 1, MOBA_BLOCK), F32)]
    last = n_blocks - 1
    return pl.pallas_call(
        kern,
        grid=(B // rows, n_blocks // 2),
        in_specs=[
            pl.BlockSpec((rows, 1, A, MOBA_BLOCK), lambda b, p: (b, p, 0, 0)),
            pl.BlockSpec((rows, 1, A, MOBA_BLOCK), lambda b, p: (b, last - p, 0, 0)),
            pl.BlockSpec((rows, S, A), lambda b, p: (b, 0, 0)),
            pl.BlockSpec((rows, n_blocks, n_heads * V_ROWS, MOBA_BLOCK), lambda b, p: (b, 0, 0, 0)),
            pl.BlockSpec((rows, n_blocks, A), lambda b, p: (b, 0, 0)),
        ],
        out_specs=pl.BlockSpec((rows, 2, 1, MOBA_BLOCK, A), lambda b, p: (b, 0, p, 0, 0)),
        out_shape=jax.ShapeDtypeStruct((B, 2, n_blocks // 2, MOBA_BLOCK, A), BF16),
        scratch_shapes=[
            pltpu.VMEM((2 * n_items, LANES, MOBA_BLOCK), BF16),
            pltpu.VMEM((rows * A // LANES, 2 * n_blocks, LANES), BF16),
            pltpu.VMEM((2, rows, n_heads * n_blocks, MOBA_BLOCK), F32),
            pltpu.VMEM((2 * n_items, 1, MOBA_BLOCK), F32),
            pltpu.VMEM((2 * n_items, V_ROWS, MOBA_BLOCK), F32),
        ] + stage + stage,
        compiler_params=pltpu.CompilerParams(dimension_semantics=("parallel", "arbitrary"),
                                             vmem_limit_bytes=VMEM_LIMIT),
        name="moba_attn",
    )(qT, qT, k, vT, km)


def _out_mlp_kernel(x_ref, attn_ref, pool_ref, mod_ref, g_post_ref, g_pre2_ref, g_post2_ref,
                    wo_ref, wup_ref, wdn_ref, o_ref, *, sub_rows, ff_chunk):
    A = attn_ref.shape[4]
    n_groups = x_ref.shape[1] // sub_rows
    half = pl.program_id(1)
    groups = [slice(r * sub_rows, (r + 1) * sub_rows) for r in range(n_groups)]
    ys = [jnp.dot(attn_ref[0, 0, jnp.where(half == 0, r, n_groups - 1 - r)], wo_ref[:A, :],
                  preferred_element_type=F32)
          + jnp.dot(pool_ref[0, rows, :], wo_ref[A:, :], preferred_element_type=F32)
          for r, rows in enumerate(groups)]
    gain1 = mod_ref[0, 2:3, :] * g_post_ref[...]
    gain2 = g_pre2_ref[...] * (1.0 + mod_ref[0, 4:5, :])
    gain3 = mod_ref[0, 5:6, :] * g_post2_ref[...]
    for g0 in range(0, n_groups, 2):
        rows = slice(g0 * sub_rows, (g0 + 2) * sub_rows)
        x1 = x_ref[0, rows, :] + _rms(jnp.concatenate(ys[g0:g0 + 2], axis=0)) * gain1
        hb = (_rms(x1) * gain2 + mod_ref[0, 3:4, :]).astype(BF16)
        y2 = jnp.zeros_like(x1)
        for c in range(wup_ref.shape[1] // ff_chunk):
            cols = slice(c * ff_chunk, (c + 1) * ff_chunk)
            up = jnp.dot(hb, wup_ref[:, cols], preferred_element_type=F32)
            act = jnp.square(jnp.maximum(up, 0.0)).astype(BF16)
            y2 = y2 + jnp.dot(act, wdn_ref[cols, :], preferred_element_type=F32)
        o_ref[0, rows, :] = x1 + _rms(y2) * gain3


def _out_mlp(x, attn, pool, mod3, g_post, g_pre2, g_post2, wo, wup, wdn, *, tm, sub_rows,
             ff_chunk):
    B, S, D = x.shape
    A, P, FF = attn.shape[4], pool.shape[2], wup.shape[1]
    assert attn.shape[1:4] == (S // tm, tm // sub_rows, sub_rows)
    const = lambda *shape: pl.BlockSpec(shape, lambda b, i: (0,) * len(shape),
                                        pipeline_mode=pl.Buffered(1))
    return pl.pallas_call(
        functools.partial(_out_mlp_kernel, sub_rows=sub_rows, ff_chunk=ff_chunk),
        grid=(B, S // tm),
        in_specs=[
            pl.BlockSpec((1, tm, D), lambda b, i: (b, i, 0)),
            pl.BlockSpec((1, 1) + attn.shape[2:], lambda b, i: (b, i, 0, 0, 0)),
            pl.BlockSpec((1, tm, P), lambda b, i: (b, i, 0)),
            pl.BlockSpec((1, 6, D), lambda b, i: (b, 0, 0)),
            const(1, D), const(1, D), const(1, D),
            const(A + P, D), const(D, FF), const(FF, D),
        ],
        out_specs=pl.BlockSpec((1, tm, D), lambda b, i: (b, i, 0)),
        out_shape=jax.ShapeDtypeStruct((B, S, D), F32),
        compiler_params=pltpu.CompilerParams(dimension_semantics=("parallel", "parallel"),
                                             vmem_limit_bytes=VMEM_LIMIT),
        name="out_mlp",
    )(x, attn, pool, mod3, g_post, g_pre2, g_post2, wo, wup, wdn)


def _rope_tables(S):
    inv_freq = 1.0 / (ROPE_THETA ** (np.arange(HALF, dtype=np.float64) * (2.0 / HEAD_DIM)))
    ang = np.arange(S, dtype=np.float64)[:, None] * inv_freq[None, :]
    cos, sin = np.cos(ang).astype(np.float32), np.sin(ang).astype(np.float32)
    zero = np.zeros_like(sin)
    reps = LANES // HEAD_DIM
    cosn = np.tile(np.concatenate([cos, cos], axis=1), (1, reps))
    sina = np.tile(np.concatenate([-sin, zero], axis=1), (1, reps))
    sinb = np.tile(np.concatenate([zero, sin], axis=1), (1, reps))
    return tuple(jnp.asarray(t) for t in (cosn, sina, sinb, np.ascontiguousarray(cos.T),
                                          np.ascontiguousarray(sin.T)))


def kernel(x, c, w_ada, b_ada, g_mix_pre, g_mix_post, w_in, w_pool, pool_scale, w_out,
           g_mlp_pre, g_mlp_post, w_up, w_down):
    B, S, D = x.shape
    depth = w_ada.shape[0]
    P = pool_scale.shape[1]
    A = w_out.shape[1] - P
    tables = _rope_tables(S)
    for l in range(depth):
        mod3 = _adaln_mod(c, w_ada[l], b_ada[l]).reshape(B, 6, D)
        w_in_b = w_in[l].astype(BF16)
        wqT = w_in_b[:, :A].T
        wk = w_in_b[:, A:2 * A]
        wvT = w_in_b[:, 2 * A:3 * A].T
        wu = w_in_b[:, 3 * A:]
        qT, k, vT, pool_in, km = _mix_in(x, mod3, g_mix_pre[l][None, :], wqT, wk, wvT, wu, tables,
                                         tm=1024, sub_rows=256)
        wo = jnp.concatenate([w_out[l][:A], _fold_pool(w_pool[l], pool_scale[l][None, :],
                                                      w_out[l], A)], axis=0).astype(BF16)
        attn = _moba_attn(qT, k, vT, km, rows=2 if B % 2 == 0 else 1)
        x = _out_mlp(x, attn, pool_in, mod3, g_mix_post[l][None, :], g_mlp_pre[l][None, :],
                     g_mlp_post[l][None, :], wo, w_up[l].astype(BF16),
                     w_down[l].astype(BF16), tm=S // 2, sub_rows=MOBA_BLOCK, ff_chunk=1024)
    return x
```
